```python
import math
import jax, jax.numpy as jnp
from jax import lax
import numpy as np

D_MODEL = 1024
BATCH = 2
SEQ = 16384
DEPTH = 1
DEC_BATCH = 16
DEC_SEQ = 64
PAST_LEN = 4096

CHUNK = 64
GLA_HEADS = 4
GLA_DK = 64
GLA_DV = 128
GLA_KEY = GLA_HEADS * GLA_DK
GLA_VAL = GLA_HEADS * GLA_DV
GLA_GATE_RANK = 16
GLA_GATE_TAU = 16.0
S5_WIDTH = 512
S5_GROUP = 16
S5_GROUPS = S5_WIDTH // S5_GROUP
S5_STATE = 64
D_FF = 2816
EPS = 1e-6
IN_SIZES = (GLA_KEY, GLA_KEY, GLA_VAL, GLA_VAL, GLA_GATE_RANK, S5_WIDTH, D_MODEL, D_MODEL)
IN_COLS = sum(IN_SIZES)

kernel_name = "hybrid_gla_s5_macaron_stream_step"


def _rmsnorm(x, g):
    xf = x.astype(jnp.float32)
    xf = xf * lax.rsqrt(jnp.mean(xf * xf, axis=-1, keepdims=True) + EPS)
    return (xf * g.astype(jnp.float32)).astype(x.dtype)


def _swiglu(x, w_gate, w_up, w_down):
    return (jax.nn.silu(x @ w_gate) * (x @ w_up)) @ w_down


def _heads(z, n):
    b, t, _ = z.shape
    return z.reshape(b, t, n, -1).transpose(0, 2, 1, 3)


def _gla_chunked(q, k, v, log_a, s0, chunk):
    bsz, nh, t, dk = q.shape
    dv = v.shape[-1]
    n = t // chunk

    def to_chunks(z):
        return jnp.moveaxis(z.reshape(bsz, nh, n, chunk, z.shape[-1]), 2, 0)

    causal = jnp.tril(jnp.ones((chunk, chunk), dtype=bool))[:, :, None]

    def step(s, inp):
        qi, ki, vi, ai = inp
        b = jnp.cumsum(ai, axis=2)
        diff = b[:, :, :, None, :] - b[:, :, None, :, :]
        decay = jnp.exp(jnp.where(causal, diff, -jnp.inf))
        scores = jnp.einsum('bhtd,bhsd,bhtsd->bhts', qi, ki, decay)
        o = jnp.einsum('bhts,bhse->bhte', scores, vi) + jnp.einsum('bhtd,bhde->bhte', qi * jnp.exp(b), s)
        b_last = b[:, :, -1:, :]
        s_new = jnp.exp(b_last[:, :, 0, :])[..., None] * s + jnp.einsum('bhsd,bhse->bhde', ki * jnp.exp(b_last - b), vi)
        return s_new, o

    s_fin, oc = lax.scan(step, s0, (to_chunks(q), to_chunks(k), to_chunks(v), to_chunks(log_a)))
    o = jnp.moveaxis(oc, 0, 2).reshape(bsz, nh, t, dv)
    return o, s_fin


def _s5_scan(u, lam_re, lam_im, log_dt, b_re, b_im, c_re, c_im, d_skip, x0):
    f32 = jnp.float32
    bsz, t, _ = u.shape
    uf = u.astype(f32)
    ug = uf.reshape(bsz, t, S5_GROUPS, S5_GROUP)
    lr, li = lam_re.astype(f32), lam_im.astype(f32)
    dt = jnp.exp(log_dt.astype(f32))[:, None]
    mag = jnp.exp(lr * dt)
    ab_re, ab_im = mag * jnp.cos(li * dt), mag * jnp.sin(li * dt)
    nr, ni = ab_re - 1.0, ab_im
    den = lr * lr + li * li
    f_re = (nr * lr + ni * li) / den
    f_im = (ni * lr - nr * li) / den
    br, bi = b_re.astype(f32), b_im.astype(f32)
    bb_re = f_re[..., None] * br - f_im[..., None] * bi
    bb_im = f_re[..., None] * bi + f_im[..., None] * br
    bu_re = jnp.einsum('btgh,gph->btgp', ug, bb_re)
    bu_im = jnp.einsum('btgh,gph->btgp', ug, bb_im)
    x0r, x0i = x0[..., 0].astype(f32), x0[..., 1].astype(f32)
    bu_re = bu_re.at[:, 0].add(ab_re * x0r - ab_im * x0i)
    bu_im = bu_im.at[:, 0].add(ab_re * x0i + ab_im * x0r)
    a_re = jnp.broadcast_to(ab_re, bu_re.shape)
    a_im = jnp.broadcast_to(ab_im, bu_im.shape)

    def combine(e1, e2):
        a1r, a1i, b1r, b1i = e1
        a2r, a2i, b2r, b2i = e2
        return (a1r * a2r - a1i * a2i,
                a1r * a2i + a1i * a2r,
                a2r * b1r - a2i * b1i + b2r,
                a2r * b1i + a2i * b1r + b2i)

    _, _, xr, xi = lax.associative_scan(combine, (a_re, a_im, bu_re, bu_im), axis=1)
    y = jnp.einsum('gjp,btgp->btgj', c_re.astype(f32), xr) - jnp.einsum('gjp,btgp->btgj', c_im.astype(f32), xi)
    y = y.reshape(bsz, t, S5_WIDTH) + d_skip.astype(f32) * uf
    x_last = jnp.stack([xr[:, -1], xi[:, -1]], axis=-1)
    return y, x_last


def _layer(x, s_gla, s_s5, chunk, p):
    f32 = jnp.float32
    bsz, t, _ = x.shape
    h = x + 0.5 * _swiglu(_rmsnorm(x, p['g_ffn1']), p['w_ffn1_gate'], p['w_ffn1_up'], p['w_ffn1_down'])
    u = _rmsnorm(h, p['g_mix'])
    z = u @ p['w_in']
    splits = [sum(IN_SIZES[:i + 1]) for i in range(len(IN_SIZES) - 1)]
    q, k, v, r, ga, us5, gm_gla, gm_s5 = jnp.split(z, splits, axis=-1)
    log_a = jax.nn.log_sigmoid((ga @ p['w_gate_up'] + p['b_gate']).astype(f32)) / GLA_GATE_TAU
    qh = _heads(q.astype(f32), GLA_HEADS) * (GLA_DK ** -0.5)
    kh = _heads(k.astype(f32), GLA_HEADS)
    vh = _heads(v.astype(f32), GLA_HEADS)
    ah = _heads(log_a, GLA_HEADS)
    o, s_gla_new = _gla_chunked(qh, kh, vh, ah, s_gla.astype(f32), chunk)
    o = o.transpose(0, 2, 1, 3)
    o = o * lax.rsqrt(jnp.mean(o * o, axis=-1, keepdims=True) + EPS)
    o = o.reshape(bsz, t, GLA_VAL) * p['g_gla_head'].astype(f32)
    y_gla = (o * jax.nn.silu(r.astype(f32))).astype(h.dtype) @ p['w_gla_out']
    ys5, s_s5_new = _s5_scan(us5, p['s5_lam_re'], p['s5_lam_im'], p['s5_log_dt'], p['s5_b_re'], p['s5_b_im'],
                             p['s5_c_re'], p['s5_c_im'], p['s5_d'], s_s5)
    g5 = jax.nn.gelu(ys5).astype(h.dtype)
    y_s5 = ((g5 @ p['w_glu_a']) * jax.nn.sigmoid(g5 @ p['w_glu_b'])) @ p['w_s5_out']
    m = jax.nn.sigmoid(gm_gla) * y_gla + jax.nn.sigmoid(gm_s5) * y_s5
    h = h + m @ p['w_out']
    h = h + 0.5 * _swiglu(_rmsnorm(h, p['g_ffn2']), p['w_ffn2_gate'], p['w_ffn2_up'], p['w_ffn2_down'])
    return h, s_gla_new, s_s5_new


def setup_inputs(seed: int = 0) -> dict:
    key = jax.random.key(seed)
    ks = jax.random.split(key, 40)
    f32 = jnp.float32

    def nrm(k, shape, scale):
        return jax.random.normal(k, shape, f32) * scale

    def gain(k, n):
        return 1.0 + 0.05 * jax.random.normal(k, (n,), f32)

    lam_im_base = math.pi * jnp.arange(S5_STATE, dtype=f32)[None, :]
    return {
        'x_prompt': nrm(ks[0], (BATCH, SEQ, D_MODEL), 1.0),
        'x_sample': nrm(ks[1], (DEC_BATCH, DEC_SEQ, D_MODEL), 1.0),
        'state_gla': nrm(ks[2], (DEC_BATCH, GLA_HEADS, GLA_DK, GLA_DV), 0.5),
        'state_s5': nrm(ks[3], (DEC_BATCH, S5_GROUPS, S5_STATE, 2), 0.5),
        'g_ffn1': gain(ks[4], D_MODEL),
        'w_ffn1_gate': nrm(ks[5], (D_MODEL, D_FF), D_MODEL ** -0.5),
        'w_ffn1_up': nrm(ks[6], (D_MODEL, D_FF), D_MODEL ** -0.5),
        'w_ffn1_down': nrm(ks[7], (D_FF, D_MODEL), D_FF ** -0.5),
        'g_mix': gain(ks[8], D_MODEL),
        'w_in': nrm(ks[9], (D_MODEL, IN_COLS), D_MODEL ** -0.5),
        'w_gate_up': nrm(ks[10], (GLA_GATE_RANK, GLA_KEY), GLA_GATE_RANK ** -0.5),
        'b_gate': nrm(ks[11], (GLA_KEY,), 0.1),
        'g_gla_head': gain(ks[12], GLA_VAL),
        'w_gla_out': nrm(ks[13], (GLA_VAL, D_MODEL), GLA_VAL ** -0.5),
        's5_lam_re': -0.5 + nrm(ks[14], (S5_GROUPS, S5_STATE), 0.01),
        's5_lam_im': lam_im_base + nrm(ks[15], (S5_GROUPS, S5_STATE), 0.01),
        's5_log_dt': jax.random.uniform(ks[16], (S5_GROUPS,), f32, math.log(1e-3), math.log(1e-1)),
        's5_b_re': nrm(ks[17], (S5_GROUPS, S5_STATE, S5_GROUP), (0.5 / S5_GROUP) ** 0.5),
        's5_b_im': nrm(ks[18], (S5_GROUPS, S5_STATE, S5_GROUP), (0.5 / S5_GROUP) ** 0.5),
        's5_c_re': nrm(ks[19], (S5_GROUPS, S5_GROUP, S5_STATE), (0.5 / S5_STATE) ** 0.5),
        's5_c_im': nrm(ks[20], (S5_GROUPS, S5_GROUP, S5_STATE), (0.5 / S5_STATE) ** 0.5),
        's5_d': nrm(ks[21], (S5_WIDTH,), 1.0),
        'w_glu_a': nrm(ks[22], (S5_WIDTH, S5_WIDTH), S5_WIDTH ** -0.5),
        'w_glu_b': nrm(ks[23], (S5_WIDTH, S5_WIDTH), S5_WIDTH ** -0.5),
        'w_s5_out': nrm(ks[24], (S5_WIDTH, D_MODEL), S5_WIDTH ** -0.5),
        'w_out': nrm(ks[25], (D_MODEL, D_MODEL), D_MODEL ** -0.5),
        'g_ffn2': gain(ks[26], D_MODEL),
        'w_ffn2_gate': nrm(ks[27], (D_MODEL, D_FF), D_MODEL ** -0.5),
        'w_ffn2_up': nrm(ks[28], (D_MODEL, D_FF), D_MODEL ** -0.5),
        'w_ffn2_down': nrm(ks[29], (D_FF, D_MODEL), D_FF ** -0.5),
        'g_final': gain(ks[30], D_MODEL),
    }


def reference(x_prompt, x_sample, state_gla, state_s5, g_ffn1, w_ffn1_gate, w_ffn1_up, w_ffn1_down,
              g_mix, w_in, w_gate_up, b_gate, g_gla_head, w_gla_out, s5_lam_re, s5_lam_im, s5_log_dt,
              s5_b_re, s5_b_im, s5_c_re, s5_c_im, s5_d, w_glu_a, w_glu_b, w_s5_out, w_out,
              g_ffn2, w_ffn2_gate, w_ffn2_up, w_ffn2_down, g_final):
    p = {
        'g_ffn1': g_ffn1, 'w_ffn1_gate': w_ffn1_gate, 'w_ffn1_up': w_ffn1_up, 'w_ffn1_down': w_ffn1_down,
        'g_mix': g_mix, 'w_in': w_in, 'w_gate_up': w_gate_up, 'b_gate': b_gate,
        'g_gla_head': g_gla_head, 'w_gla_out': w_gla_out,
        's5_lam_re': s5_lam_re, 's5_lam_im': s5_lam_im, 's5_log_dt': s5_log_dt,
        's5_b_re': s5_b_re, 's5_b_im': s5_b_im, 's5_c_re': s5_c_re, 's5_c_im': s5_c_im, 's5_d': s5_d,
        'w_glu_a': w_glu_a, 'w_glu_b': w_glu_b, 'w_s5_out': w_s5_out, 'w_out': w_out,
        'g_ffn2': g_ffn2, 'w_ffn2_gate': w_ffn2_gate, 'w_ffn2_up': w_ffn2_up, 'w_ffn2_down': w_ffn2_down,
    }
    bp = x_prompt.shape[0]
    hp, gla_p, s5_p = x_prompt, jnp.zeros((bp, GLA_HEADS, GLA_DK, GLA_DV), jnp.float32), jnp.zeros((bp, S5_GROUPS, S5_STATE, 2), jnp.float32)
    hs, gla_s, s5_s = x_sample, state_gla, state_s5
    for _ in range(DEPTH):
        hp, gla_p, s5_p = _layer(hp, gla_p, s5_p, CHUNK, p)
        hs, gla_s, s5_s = _layer(hs, gla_s, s5_s, x_sample.shape[1], p)
    y_prompt = _rmsnorm(hp, g_final)
    y_sample = _rmsnorm(hs, g_final)
    return (y_prompt, y_sample, gla_p, s5_p, gla_s, s5_s)
```

```python
import functools
import math

import jax
import jax.numpy as jnp
from jax import lax
from jax.experimental import pallas as pl
from jax.experimental.pallas import tpu as pltpu

F32 = jnp.float32
BF16 = jnp.bfloat16

D_MODEL = 1024
D_FF = 2816
CHUNK = 64
SUB = 16
GLA_HEADS = 4
GLA_DK = 64
GLA_DV = 128
GLA_KEY = GLA_HEADS * GLA_DK
GLA_VAL = GLA_HEADS * GLA_DV
GLA_GATE_RANK = 16
GLA_GATE_TAU = 16.0
S5_WIDTH = 512
S5_GROUP = 16
S5_GROUPS = 32
S5_STATE = 64
S5_MODES = S5_GROUPS * S5_STATE
S5_HALF_CH = S5_WIDTH // 2
S5_HALF_MODES = S5_MODES // 2
EPS = 1e-6
SUBLANES = 8
LANES = 128
EXP_CLAMP = 80.0
FF_CHUNK = 512
VMEM_LIMIT = 60 * 1024 * 1024

_OQ, _OK, _OV, _OR, _OU, _OGG, _OGS = 0, 256, 512, 1024, 1536, 2048, 3072
W_MAIN_COLS = 4096


def _dot(a, b):
    return jnp.dot(a, b, preferred_element_type=F32)


def _dot_nt(a, b):
    return lax.dot_general(a, b, (((1,), (1,)), ((), ())), preferred_element_type=F32)


def _sigmoid(x):
    return 1.0 / (1.0 + jnp.exp(-x))


def _rms(x, g):
    ms = jnp.mean(x * x, axis=-1, keepdims=True)
    return x * lax.rsqrt(ms + EPS) * g


def _ff_chunks():
    out, c0 = [], 0
    while c0 < D_FF:
        n = min(FF_CHUNK, D_FF - c0)
        out.append((c0, n))
        c0 += n
    return out


def _swiglu(xn, wg_ref, wu_ref, wd_ref):
    acc = None
    for c0, n in _ff_chunks():
        g = _dot(xn, wg_ref[:, c0:c0 + n])
        u = _dot(xn, wu_ref[:, c0:c0 + n])
        a = (g * _sigmoid(g) * u).astype(BF16)
        part = _dot(a, wd_ref[c0:c0 + n, :])
        acc = part if acc is None else acc + part
    return acc


def _front_kernel(x_ref, g1_ref, wg_ref, wu_ref, wd_ref, gmix_ref, win_ref, wga_ref, wgu_ref, bg_ref,
                  h_ref, q_ref, k_ref, v_ref, r_ref, u5_ref, gmg_ref, gms_ref, la_ref):
    x = x_ref[...]
    xn = _rms(x, g1_ref[...]).astype(BF16)
    h = x + 0.5 * _swiglu(xn, wg_ref, wu_ref, wd_ref)
    h_ref[...] = h
    un = _rms(h, gmix_ref[...]).astype(BF16)

    def proj(c0, n):
        return _dot(un, win_ref[:, c0:c0 + n])

    q_ref[...] = proj(_OQ, GLA_KEY) * (GLA_DK ** -0.5)
    k_ref[...] = proj(_OK, GLA_KEY)
    v_ref[...] = proj(_OV, GLA_VAL).astype(BF16)
    r_ref[...] = proj(_OR, GLA_VAL)
    u5_ref[...] = proj(_OU, S5_WIDTH)
    gmg_ref[...] = proj(_OGG, D_MODEL)
    gms_ref[...] = proj(_OGS, D_MODEL)
    ga = _dot(un, wga_ref[...])
    pre = _dot(ga.astype(BF16), wgu_ref[...]) + bg_ref[...]
    log_sig = jnp.minimum(pre, 0.0) - jnp.log(1.0 + jnp.exp(-jnp.abs(pre)))
    la_ref[...] = log_sig * (1.0 / GLA_GATE_TAU)


def _const_spec(shape):
    nd = len(shape)
    return pl.BlockSpec(shape, lambda *_: (0,) * nd, pipeline_mode=pl.Buffered(1))


def _front(x2d, p, tm):
    n = x2d.shape[0]
    row = lambda w: pl.BlockSpec((tm, w), lambda i: (i, 0))
    in_specs = [row(D_MODEL), _const_spec((1, D_MODEL)),
                _const_spec((D_MODEL, D_FF)), _const_spec((D_MODEL, D_FF)), _const_spec((D_FF, D_MODEL)),
                _const_spec((1, D_MODEL)), _const_spec((D_MODEL, W_MAIN_COLS)),
                _const_spec((D_MODEL, LANES)), _const_spec((LANES, GLA_KEY)), _const_spec((1, GLA_KEY))]
    widths = [(D_MODEL, F32), (GLA_KEY, F32), (GLA_KEY, F32), (GLA_VAL, BF16), (GLA_VAL, F32),
              (S5_WIDTH, F32), (D_MODEL, F32), (D_MODEL, F32), (GLA_KEY, F32)]
    return pl.pallas_call(
        _front_kernel,
        out_shape=[jax.ShapeDtypeStruct((n, w), dt) for w, dt in widths],
        grid=(n // tm,),
        in_specs=in_specs,
        out_specs=[row(w) for w, _ in widths],
        compiler_params=pltpu.CompilerParams(dimension_semantics=("parallel",), vmem_limit_bytes=VMEM_LIMIT),
        name="front",
    )(x2d, p['g_ffn1'], p['wg1'], p['wu1'], p['wd1'], p['g_mix'], p['w_main'], p['w_ga'], p['w_gu'], p['b_gate'])


def _s5_prep_kernel(lr_ref, li_ref, ldt_ref, br_ref, bi_ref, pwr_ref, pwi_ref, bbr_ref, bbi_ref):
    lr, li = lr_ref[...], li_ref[...]
    dt = jnp.exp(ldt_ref[...])
    kk = (lax.broadcasted_iota(jnp.int32, (1, SUBLANES), 1) + 1).astype(F32)
    mag = jnp.exp((lr * dt) * kk)
    ang = (li * dt) * kk
    pwr_ref[...] = mag * jnp.cos(ang)
    pwi_ref[...] = mag * jnp.sin(ang)
    mag1 = jnp.exp(lr * dt)
    ab_re, ab_im = mag1 * jnp.cos(li * dt), mag1 * jnp.sin(li * dt)
    nr, ni = ab_re - 1.0, ab_im
    den = lr * lr + li * li
    f_re = (nr * lr + ni * li) / den
    f_im = (ni * lr - nr * li) / den
    br, bi = br_ref[...], bi_ref[...]
    bbr_ref[...] = f_re * br - f_im * bi
    bbi_ref[...] = f_re * bi + f_im * br


def _s5_prep(lam_re, lam_im, log_dt, b_re, b_im):
    m = S5_MODES
    col = lambda a: a.astype(F32).reshape(m, 1)
    ldt = jnp.broadcast_to(log_dt.astype(F32)[:, None], (S5_GROUPS, S5_STATE))
    outs = [jax.ShapeDtypeStruct((m, SUBLANES), F32)] * 2 + [jax.ShapeDtypeStruct((m, S5_GROUP), F32)] * 2
    return pl.pallas_call(_s5_prep_kernel, out_shape=outs, name="s5_prep")(
        col(lam_re), col(lam_im), col(ldt), b_re.astype(F32).reshape(m, S5_GROUP), b_im.astype(F32).reshape(m, S5_GROUP))


def _s5_tables(pw_re, pw_im, bb_re, bb_im, c_re, c_im):
    hg = S5_GROUPS // 2
    eye = jnp.eye(hg, dtype=F32)

    def bdiag_in(bb):
        b = bb.reshape(2, hg, S5_STATE, S5_GROUP)
        full = b.transpose(0, 1, 3, 2)[:, :, :, None, :] * eye[None, :, None, :, None]
        return full.reshape(2, S5_HALF_CH, S5_HALF_MODES)

    def bdiag_out(c):
        cc = c.astype(F32).reshape(2, hg, S5_GROUP, S5_STATE)
        full = cc.transpose(0, 1, 3, 2)[:, :, :, None, :] * eye[None, :, None, :, None]
        return full.reshape(2, S5_HALF_MODES, S5_HALF_CH)

    bbd = jnp.concatenate([bdiag_in(bb_re.reshape(S5_GROUPS, S5_STATE, S5_GROUP)),
                           bdiag_in(bb_im.reshape(S5_GROUPS, S5_STATE, S5_GROUP))], axis=2).astype(BF16)
    cbd = jnp.concatenate([bdiag_out(c_re), -bdiag_out(c_im)], axis=1).astype(BF16)

    pr = pw_re.T.reshape(SUBLANES, 2, S5_HALF_MODES)
    pi = pw_im.T.reshape(SUBLANES, 2, S5_HALF_MODES)
    pos = jnp.arange(SUBLANES)[:, None, None]
    tabs = []
    for k in (1, 2, 4):
        keep = (pos >= k).astype(F32)
        tabs += [keep * pr[k - 1][None], keep * pi[k - 1][None]]
    tabs += [pr, pi]
    tab = jnp.stack(tabs, axis=0).transpose(2, 0, 1, 3)
    return bbd, cbd, tab


def _mix_kernel(q_ref, k_ref, la_ref, v_ref, r_ref, u5_ref, st0_ref, s50_ref, gh_ref,
                bbd_ref, cbd_ref, d_ref, tab_ref,
                og_ref, g5_ref, sto_ref, s5o_ref,
                st_sc, car_sc, o_sc, xs_sc, *, tb, unroll):
    t = pl.program_id(1)
    hm = S5_HALF_MODES

    @pl.when(t == 0)
    def _():
        st_sc[...] = st0_ref[0]
        car_sc[...] = jnp.broadcast_to(s50_ref[0], (SUBLANES, 2 * S5_MODES))

    ti = lax.broadcasted_iota(jnp.int32, (CHUNK, CHUNK), 0)
    si = lax.broadcasted_iota(jnp.int32, (CHUNK, CHUNK), 1)
    causal = si <= ti
    tril = jnp.where(causal, 1.0, 0.0).astype(BF16)
    lane = lax.broadcasted_iota(jnp.int32, (1, GLA_KEY), 1) % LANES
    head_mask = (lane < GLA_DK, lane >= GLA_DK)
    nsub = CHUNK // SUB

    def chunk(r0):
        rows = pl.ds(r0, CHUNK)
        la = la_ref[rows, :]
        hi = la.astype(BF16)
        lo = (la - hi.astype(F32)).astype(BF16)
        b = _dot(tril, hi) + _dot(tril, lo)
        refs = [jnp.zeros((1, GLA_KEY), F32)] + [b[SUB * i - 1:SUB * i, :] for i in range(1, nsub)]
        rb = jnp.concatenate([jnp.broadcast_to(x, (SUB, GLA_KEY)) for x in refs], axis=0)
        q, k = q_ref[rows, :], k_ref[rows, :]
        b_last = b[CHUNK - 1:CHUNK, :]
        qt = q * jnp.exp(b - rb)
        qd = q * jnp.exp(b)
        qt = [jnp.where(m, qt, 0.0).astype(BF16) for m in head_mask]
        qd = [jnp.where(m, qd, 0.0).astype(BF16) for m in head_mask]
        kd = (k * jnp.exp(b_last - b)).astype(BF16)
        e_last = jnp.exp(b_last)
        kts = [(k * jnp.exp(jnp.minimum(x - b, EXP_CLAMP))).astype(BF16) for x in refs]
        for h in range(GLA_HEADS):
            pc = slice(LANES * (h // 2), LANES * (h // 2 + 1))
            qt_h = qt[h % 2][:, pc]
            sc = jnp.concatenate(
                [_dot_nt(qt_h[SUB * i:SUB * (i + 1)], kts[i][:, pc]) for i in range(nsub)], axis=0)
            pm = jnp.where(causal, sc, 0.0).astype(BF16)
            vh = v_ref[rows, GLA_DV * h:GLA_DV * (h + 1)]
            st = st_sc[h]
            o = _dot(pm, vh) + _dot_nt(qd[h % 2][:, pc], st.astype(BF16))
            o_sc[rows, GLA_DV * h:GLA_DV * (h + 1)] = o
            vt = vh.astype(F32).T.astype(BF16)
            st_sc[h] = st * e_last[:, pc] + _dot(vt, kd[:, pc])

    nchunks = tb // CHUNK
    if nchunks <= unroll:
        for c in range(nchunks):
            chunk(c * CHUNK)
    else:
        def body(i, carry):
            for j in range(unroll):
                chunk(pl.multiple_of((i * unroll + j) * CHUNK, CHUNK))
            return carry
        lax.fori_loop(0, nchunks // unroll, body, 0)

    o_all = o_sc[...]
    normed = []
    for h in range(GLA_HEADS):
        oh = o_all[:, GLA_DV * h:GLA_DV * (h + 1)]
        normed.append(oh * lax.rsqrt(jnp.mean(oh * oh, axis=-1, keepdims=True) + EPS))
    r = r_ref[...]
    og = jnp.concatenate(normed, axis=1) * gh_ref[...] * (r * _sigmoid(r))
    og_ref[...] = og.astype(BF16)
    sto_ref[0] = st_sc[...]

    ng = tb // SUBLANES
    for hf in range(2):
        cs = slice(S5_HALF_CH * hf, S5_HALF_CH * (hf + 1))
        u = u5_ref[:, cs]
        bu = _dot(u.astype(BF16), bbd_ref[hf])
        xr = bu[:, :hm].reshape(ng, SUBLANES, hm)
        xi = bu[:, hm:].reshape(ng, SUBLANES, hm)
        for s, kstep in enumerate((1, 2, 4)):
            ar, ai = tab_ref[hf, 2 * s], tab_ref[hf, 2 * s + 1]
            sr = pltpu.roll(xr, kstep, axis=1)
            sim = pltpu.roll(xi, kstep, axis=1)
            xr, xi = xr + ar * sr - ai * sim, xi + ar * sim + ai * sr
        pr, pi = tab_ref[hf, 6], tab_ref[hf, 7]
        base = 2 * hm * hf
        cr = car_sc[:, base:base + hm]
        ci = car_sc[:, base + hm:base + 2 * hm]
        for g in range(ng):
            yr = xr[g] + pr * cr - pi * ci
            yi = xi[g] + pr * ci + pi * cr
            xs_sc[SUBLANES * g:SUBLANES * (g + 1), :hm] = yr
            xs_sc[SUBLANES * g:SUBLANES * (g + 1), hm:] = yi
            cr = jnp.broadcast_to(yr[SUBLANES - 1:SUBLANES], (SUBLANES, hm))
            ci = jnp.broadcast_to(yi[SUBLANES - 1:SUBLANES], (SUBLANES, hm))
        car_sc[:, base:base + hm] = cr
        car_sc[:, base + hm:base + 2 * hm] = ci
        ys = _dot(xs_sc[...].astype(BF16), cbd_ref[hf]) + d_ref[:, cs] * u
        inner = math.sqrt(2.0 / math.pi) * (ys + 0.044715 * (ys * ys * ys))
        g5_ref[:, cs] = (0.5 * ys * (1.0 + jnp.tanh(inner))).astype(BF16)
    s5o_ref[0] = car_sc[0:1, :]


def _mix(q, k, la, v, r, u5, st0, s50, p, nb, tb, unroll):
    n = q.shape[0]
    nt = n // (nb * tb)
    row = lambda w: pl.BlockSpec((tb, w), lambda b, t: (b * nt + t, 0))
    st_spec = pl.BlockSpec((1, GLA_HEADS, GLA_DV, LANES), lambda b, t: (b, 0, 0, 0))
    s5_spec = pl.BlockSpec((1, 1, 2 * S5_MODES), lambda b, t: (b, 0, 0))
    in_specs = [row(GLA_KEY), row(GLA_KEY), row(GLA_KEY), row(GLA_VAL), row(GLA_VAL), row(S5_WIDTH),
                st_spec, s5_spec, _const_spec((1, GLA_VAL)),
                _const_spec((2, S5_HALF_CH, 2 * S5_HALF_MODES)), _const_spec((2, 2 * S5_HALF_MODES, S5_HALF_CH)),
                _const_spec((1, S5_WIDTH)), _const_spec((2, 8, SUBLANES, S5_HALF_MODES))]
    out_shape = [jax.ShapeDtypeStruct((n, GLA_VAL), BF16), jax.ShapeDtypeStruct((n, S5_WIDTH), BF16),
                 jax.ShapeDtypeStruct((nb, GLA_HEADS, GLA_DV, LANES), F32),
                 jax.ShapeDtypeStruct((nb, 1, 2 * S5_MODES), F32)]
    return pl.pallas_call(
        functools.partial(_mix_kernel, tb=tb, unroll=unroll),
        out_shape=out_shape,
        grid=(nb, nt),
        in_specs=in_specs,
        out_specs=[row(GLA_VAL), row(S5_WIDTH), st_spec, s5_spec],
        scratch_shapes=[pltpu.VMEM((GLA_HEADS, GLA_DV, LANES), F32),
                        pltpu.VMEM((SUBLANES, 2 * S5_MODES), F32),
                        pltpu.VMEM((tb, GLA_VAL), F32),
                        pltpu.VMEM((tb, 2 * S5_HALF_MODES), F32)],
        compiler_params=pltpu.CompilerParams(dimension_semantics=("arbitrary", "arbitrary"),
                                             vmem_limit_bytes=VMEM_LIMIT),
        name="mix",
    )(q, k, la, v, r, u5, st0, s50, p['g_gla_head'], p['bbd'], p['cbd'], p['s5_d'], p['tab'])


def _back_kernel(h_ref, og_ref, g5_ref, gmg_ref, gms_ref, wgo_ref, wa_ref, wb_ref, wso_ref, wo_ref,
                 g2_ref, wg_ref, wu_ref, wd_ref, gf_ref, y_ref):
    y_gla = _dot(og_ref[...], wgo_ref[...])
    g5 = g5_ref[...]
    glu = _dot(g5, wa_ref[...]) * _sigmoid(_dot(g5, wb_ref[...]))
    y_s5 = _dot(glu.astype(BF16), wso_ref[...])
    m = _sigmoid(gmg_ref[...]) * y_gla + _sigmoid(gms_ref[...]) * y_s5
    h = h_ref[...] + _dot(m.astype(BF16), wo_ref[...])
    hn = _rms(h, g2_ref[...]).astype(BF16)
    h = h + 0.5 * _swiglu(hn, wg_ref, wu_ref, wd_ref)
    y_ref[...] = _rms(h, gf_ref[...])


def _back(h, og, g5, gmg, gms, p, tm):
    n = h.shape[0]
    row = lambda w: pl.BlockSpec((tm, w), lambda i: (i, 0))
    in_specs = [row(D_MODEL), row(GLA_VAL), row(S5_WIDTH), row(D_MODEL), row(D_MODEL),
                _const_spec((GLA_VAL, D_MODEL)), _const_spec((S5_WIDTH, S5_WIDTH)), _const_spec((S5_WIDTH, S5_WIDTH)),
                _const_spec((S5_WIDTH, D_MODEL)), _const_spec((D_MODEL, D_MODEL)), _const_spec((1, D_MODEL)),
                _const_spec((D_MODEL, D_FF)), _const_spec((D_MODEL, D_FF)), _const_spec((D_FF, D_MODEL)),
                _const_spec((1, D_MODEL))]
    return pl.pallas_call(
        _back_kernel,
        out_shape=jax.ShapeDtypeStruct((n, D_MODEL), F32),
        grid=(n // tm,),
        in_specs=in_specs,
        out_specs=row(D_MODEL),
        compiler_params=pltpu.CompilerParams(dimension_semantics=("parallel",), vmem_limit_bytes=VMEM_LIMIT),
        name="back",
    )(h, og, g5, gmg, gms, p['w_gla_out'], p['w_glu_a'], p['w_glu_b'], p['w_s5_out'], p['w_out'],
      p['g_ffn2'], p['wg2'], p['wu2'], p['wd2'], p['g_final'])


def _gla_state_in(s):
    st = jnp.swapaxes(s.astype(F32), 2, 3)
    z = jnp.zeros_like(st)
    even = jnp.concatenate([st, z], axis=-1)
    odd = jnp.concatenate([z, st], axis=-1)
    sel = (jnp.arange(GLA_HEADS) % 2 == 0)[None, :, None, None]
    return jnp.where(sel, even, odd)


def _gla_state_out(st):
    even, odd = st[..., :GLA_DK], st[..., GLA_DK:]
    sel = (jnp.arange(GLA_HEADS) % 2 == 0)[None, :, None, None]
    return jnp.swapaxes(jnp.where(sel, even, odd), 2, 3)


def _s5_state_in(x0):
    b = x0.shape[0]
    x = x0.astype(F32).reshape(b, 2, S5_HALF_MODES, 2)
    return jnp.swapaxes(x, 2, 3).reshape(b, 1, 2 * S5_MODES)


def _s5_state_out(x):
    b = x.shape[0]
    y = x.reshape(b, 2, 2, S5_HALF_MODES)
    return jnp.swapaxes(y, 2, 3).reshape(b, S5_GROUPS, S5_STATE, 2)


def _layer(x, s_gla, s_s5, p, tm, tb, unroll):
    bsz, t, _ = x.shape
    n = bsz * t
    h, q, k, v, r, u5, gmg, gms, la = _front(x.reshape(n, D_MODEL), p, tm)
    og, g5, st, s5 = _mix(q, k, la, v, r, u5, _gla_state_in(s_gla), _s5_state_in(s_s5), p, bsz, tb, unroll)
    y = _back(h, og, g5, gmg, gms, p, tm)
    return y.reshape(bsz, t, D_MODEL), _gla_state_out(st), _s5_state_out(s5)


def kernel(x_prompt, x_sample, state_gla, state_s5, g_ffn1, w_ffn1_gate, w_ffn1_up, w_ffn1_down, g_mix, w_in, w_gate_up, b_gate, g_gla_head, w_gla_out, s5_lam_re, s5_lam_im, s5_log_dt, s5_b_re, s5_b_im, s5_c_re, s5_c_im, s5_d, w_glu_a, w_glu_b, w_s5_out, w_out, g_ffn2, w_ffn2_gate, w_ffn2_up, w_ffn2_down, g_final):
    vec = lambda a: a.astype(F32).reshape(1, -1)
    sizes = (GLA_KEY, GLA_KEY, GLA_VAL, GLA_VAL, GLA_GATE_RANK, S5_WIDTH, D_MODEL, D_MODEL)
    offs = [sum(sizes[:i]) for i in range(len(sizes) + 1)]
    wq, wk, wv, wr, wga, wu5, wgg, wgs = [w_in[:, offs[i]:offs[i + 1]] for i in range(len(sizes))]
    pw_re, pw_im, bb_re, bb_im = _s5_prep(s5_lam_re, s5_lam_im, s5_log_dt, s5_b_re, s5_b_im)
    bbd, cbd, tab = _s5_tables(pw_re, pw_im, bb_re, bb_im, s5_c_re, s5_c_im)
    p = {
        'g_ffn1': vec(g_ffn1), 'wg1': w_ffn1_gate.astype(BF16), 'wu1': w_ffn1_up.astype(BF16),
        'wd1': w_ffn1_down.astype(BF16), 'g_mix': vec(g_mix),
        'w_main': jnp.concatenate([wq, wk, wv, wr, wu5, wgg, wgs], axis=1).astype(BF16),
        'w_ga': jnp.pad(wga, ((0, 0), (0, LANES - GLA_GATE_RANK))).astype(BF16),
        'w_gu': jnp.pad(w_gate_up, ((0, LANES - GLA_GATE_RANK), (0, 0))).astype(BF16),
        'b_gate': vec(b_gate), 'g_gla_head': vec(g_gla_head),
        'bbd': bbd, 'cbd': cbd, 'tab': tab, 's5_d': vec(s5_d),
        'w_gla_out': w_gla_out.astype(BF16), 'w_glu_a': w_glu_a.astype(BF16), 'w_glu_b': w_glu_b.astype(BF16),
        'w_s5_out': w_s5_out.astype(BF16), 'w_out': w_out.astype(BF16),
        'g_ffn2': vec(g_ffn2), 'wg2': w_ffn2_gate.astype(BF16), 'wu2': w_ffn2_up.astype(BF16),
        'wd2': w_ffn2_down.astype(BF16), 'g_final': vec(g_final),
    }
    bp = x_prompt.shape[0]
    zero_gla = jnp.zeros((bp, GLA_HEADS, GLA_DK, GLA_DV), F32)
    zero_s5 = jnp.zeros((bp, S5_GROUPS, S5_STATE, 2), F32)
    y_p, gla_p, s5_p = _layer(x_prompt, zero_gla, zero_s5, p, tm=256, tb=512, unroll=2)
    y_s, gla_s, s5_s = _layer(x_sample, state_gla, state_s5, p, tm=256, tb=x_sample.shape[1], unroll=2)
    return (y_p, y_s, gla_p, s5_p, gla_s, s5_s)
```

```python
import functools
import math

import jax
import jax.numpy as jnp
from jax import lax
from jax.experimental import pallas as pl
from jax.experimental.pallas import tpu as pltpu

F32 = jnp.float32
BF16 = jnp.bfloat16

D_MODEL = 1024
D_FF = 2816
CHUNK = 64
SUB = 16
GLA_HEADS = 4
GLA_DK = 64
GLA_DV = 128
GLA_KEY = GLA_HEADS * GLA_DK
GLA_VAL = GLA_HEADS * GLA_DV
GLA_GATE_RANK = 16
GLA_GATE_TAU = 16.0
S5_WIDTH = 512
S5_GROUP = 16
S5_GROUPS = 32
S5_STATE = 64
S5_MODES = S5_GROUPS * S5_STATE
EPS = 1e-6
SUBLANES = 8
LANES = 128
MXU_TILE = 256
S5_STEP = SUBLANES
S5_PAIRS = S5_GROUPS // 2
PAIR_CH = 2 * S5_GROUP
S5_ROW = S5_STEP * S5_WIDTH
N_SCAN_TABLES = 8
EXP_CLAMP = 80.0
FF_CHUNK = 512
VMEM_LIMIT = 60 * 1024 * 1024

_OQ, _OK, _OV, _OR, _OU, _OGG, _OGS = 0, 256, 512, 1024, 1536, 2048, 3072
W_MAIN_COLS = 4096


def _dot(a, b):
    return jnp.dot(a, b, preferred_element_type=F32)


def _dot_nt(a, b):
    return lax.dot_general(a, b, (((1,), (1,)), ((), ())), preferred_element_type=F32)


def _split_dot(a, b):
    a1 = a.astype(BF16)
    a2 = (a - a1.astype(F32)).astype(BF16)
    b1 = b.astype(BF16)
    b2 = (b - b1.astype(F32)).astype(BF16)
    return _dot(a1, b1) + _dot(a1, b2) + _dot(a2, b1)


def _cmul(ar, ai, br, bi):
    return ar * br - ai * bi, ar * bi + ai * br


def _sigmoid(x):
    return 1.0 / (1.0 + jnp.exp(-x))


def _rms(x, g):
    ms = jnp.mean(x * x, axis=-1, keepdims=True)
    return x * lax.rsqrt(ms + EPS) * g


def _ff_chunks():
    out, c0 = [], 0
    while c0 < D_FF:
        n = min(FF_CHUNK, D_FF - c0)
        out.append((c0, n))
        c0 += n
    return out


def _swiglu(xn, wg_ref, wu_ref, wd_ref):
    acc = None
    for c0, n in _ff_chunks():
        g = _dot(xn, wg_ref[:, c0:c0 + n])
        u = _dot(xn, wu_ref[:, c0:c0 + n])
        a = (g * _sigmoid(g) * u).astype(BF16)
        part = _dot(a, wd_ref[c0:c0 + n, :])
        acc = part if acc is None else acc + part
    return acc


def _const_spec(shape):
    nd = len(shape)
    return pl.BlockSpec(shape, lambda *_: (0,) * nd, pipeline_mode=pl.Buffered(1))


def _front_kernel(x_ref, g1_ref, wg_ref, wu_ref, wd_ref, gmix_ref, win_ref, wga_ref, wgu_ref, bg_ref,
                  h_ref, q_ref, k_ref, v_ref, r_ref, u5_ref, gmg_ref, gms_ref, la_ref):
    x = x_ref[...]
    xn = _rms(x, g1_ref[...]).astype(BF16)
    h = x + 0.5 * _swiglu(xn, wg_ref, wu_ref, wd_ref)
    h_ref[...] = h
    un = _rms(h, gmix_ref[...]).astype(BF16)

    def proj(c0, n):
        return _dot(un, win_ref[:, c0:c0 + n])

    q_ref[...] = proj(_OQ, GLA_KEY) * (GLA_DK ** -0.5)
    k_ref[...] = proj(_OK, GLA_KEY)
    v_ref[...] = proj(_OV, GLA_VAL).astype(BF16)
    r_ref[...] = proj(_OR, GLA_VAL)
    u5_ref[...] = proj(_OU, S5_WIDTH)
    gmg_ref[...] = proj(_OGG, D_MODEL)
    gms_ref[...] = proj(_OGS, D_MODEL)
    ga = _dot(un, wga_ref[...])
    pre = _dot(ga.astype(BF16), wgu_ref[...]) + bg_ref[...]
    log_sig = jnp.minimum(pre, 0.0) - jnp.log(1.0 + jnp.exp(-jnp.abs(pre)))
    la_ref[...] = log_sig * (1.0 / GLA_GATE_TAU)


def _front(x2d, p, tm):
    n = x2d.shape[0]
    row = lambda w: pl.BlockSpec((tm, w), lambda i: (i, 0))
    in_specs = [row(D_MODEL), _const_spec((1, D_MODEL)),
                _const_spec((D_MODEL, D_FF)), _const_spec((D_MODEL, D_FF)), _const_spec((D_FF, D_MODEL)),
                _const_spec((1, D_MODEL)), _const_spec((D_MODEL, W_MAIN_COLS)),
                _const_spec((D_MODEL, LANES)), _const_spec((LANES, GLA_KEY)), _const_spec((1, GLA_KEY))]
    widths = [(D_MODEL, F32), (GLA_KEY, F32), (GLA_KEY, F32), (GLA_VAL, BF16), (GLA_VAL, F32),
              (S5_WIDTH, F32), (D_MODEL, F32), (D_MODEL, F32), (GLA_KEY, F32)]
    return pl.pallas_call(
        _front_kernel,
        out_shape=[jax.ShapeDtypeStruct((n, w), dt) for w, dt in widths],
        grid=(n // tm,),
        in_specs=in_specs,
        out_specs=[row(w) for w, _ in widths],
        compiler_params=pltpu.CompilerParams(dimension_semantics=("parallel",), vmem_limit_bytes=VMEM_LIMIT),
        name="front",
    )(x2d, p['g_ffn1'], p['wg1'], p['wu1'], p['wd1'], p['g_mix'], p['w_main'], p['w_ga'], p['w_gu'], p['b_gate'])


def _s5_prep_kernel(lr_ref, li_ref, ldt_ref, b128r_ref, b128i_ref, c128r_ref, c128i_ref, b512r_ref, b512i_ref,
                    t1r_ref, t1i_ref, t3r_ref, t3i_ref, kk_ref, pwr_ref, pwi_ref):
    lr, li = lr_ref[...], li_ref[...]
    dt = jnp.exp(ldt_ref[...])
    ar, th = lr * dt, li * dt

    def powers(kvec):
        mag = jnp.exp(ar * kvec)
        ang = th * kvec
        return mag * jnp.cos(ang), mag * jnp.sin(ang)

    lane = lax.broadcasted_iota(jnp.int32, (1, LANES), 1)
    k0 = lax.shift_right_logical(lane, 4).astype(F32)
    p0r, p0i = powers(k0)
    p1r, p1i = powers(k0 + 1.0)
    ab_re, ab_im = powers(jnp.ones((1, 1), F32))
    nr, ni = ab_re - 1.0, ab_im
    den = lr * lr + li * li
    f_re = (nr * lr + ni * li) / den
    f_im = (ni * lr - nr * li) / den
    bbr, bbi = _cmul(f_re, f_im, b128r_ref[...], b128i_ref[...])
    t1r_ref[...], t1i_ref[...] = _cmul(p0r, p0i, bbr, bbi)
    cr, ci = c128r_ref[...], c128i_ref[...]
    t3r_ref[...], t3i_ref[...] = _cmul(p1r, p1i, cr, ci)
    car, cai = _cmul(p0r, p0i, cr, ci)
    row = lax.broadcasted_iota(jnp.int32, (S5_MODES, S5_WIDTH), 0)
    col = lax.broadcasted_iota(jnp.int32, (S5_MODES, S5_WIDTH), 1)
    same_group = lax.shift_right_logical(row, 6) == lax.shift_right_logical(col, 4)
    dbr, dbi = _cmul(f_re, f_im, b512r_ref[...], b512i_ref[...])
    dbr = jnp.where(same_group, dbr, 0.0)
    dbi = jnp.where(same_group, dbi, 0.0)
    kk_ref[...] = _split_dot(car.T, dbr) - _split_dot(cai.T, dbi)
    k8 = ((lax.broadcasted_iota(jnp.int32, (1, SUBLANES), 1) + 1) * S5_STEP).astype(F32)
    pwr_ref[...], pwi_ref[...] = powers(k8)


def _s5_prep(lam_re, lam_im, log_dt, b_re, b_im, c_re, c_im):
    m = S5_MODES
    col = lambda a: a.astype(F32).reshape(m, 1)
    ldt = jnp.broadcast_to(log_dt.astype(F32)[:, None], (S5_GROUPS, S5_STATE))
    bt = lambda a, reps: jnp.tile(a.astype(F32).reshape(m, S5_GROUP), (1, reps))
    ct = lambda a: jnp.tile(a.astype(F32).transpose(0, 2, 1).reshape(m, S5_GROUP), (1, S5_STEP))
    shapes = [(m, LANES)] * 4 + [(LANES, S5_WIDTH)] + [(m, SUBLANES)] * 2
    return pl.pallas_call(
        _s5_prep_kernel,
        out_shape=[jax.ShapeDtypeStruct(s, F32) for s in shapes],
        compiler_params=pltpu.CompilerParams(vmem_limit_bytes=VMEM_LIMIT),
        name="s5_prep",
    )(col(lam_re), col(lam_im), col(ldt), bt(b_re, S5_STEP), bt(b_im, S5_STEP), ct(c_re), ct(c_im),
      bt(b_re, S5_GROUPS), bt(b_im, S5_GROUPS))


def _s5_tables(t1r, t1i, t3r, t3i, kk, pwr, pwi):
    np_, gl = S5_PAIRS, 2
    eye = jnp.eye(gl, dtype=F32)

    def w1_part(t):
        x = t.reshape(np_, gl, S5_STATE, S5_STEP, S5_GROUP)[:, :, :, ::-1, :]
        x = x.transpose(0, 3, 1, 4, 2)
        return x[:, :, :, :, None, :] * eye[None, None, :, None, :, None]

    w1 = jnp.stack([w1_part(t1r), w1_part(t1i)], axis=4).reshape(np_, MXU_TILE, MXU_TILE)

    def w3x_part(t):
        x = t.reshape(np_, gl, S5_STATE, S5_STEP, S5_GROUP)
        return x[:, :, :, :, None, :] * eye[None, :, None, None, :, None]

    w3x = jnp.stack([w3x_part(t3r), -w3x_part(t3i)], axis=1).reshape(np_, MXU_TILE, MXU_TILE)

    k5 = kk.reshape(S5_STEP, S5_GROUP, np_, gl, S5_GROUP)
    ii = jnp.arange(S5_STEP)[:, None]
    jj = jnp.arange(S5_STEP)[None, :]
    causal = (jj >= ii).astype(F32)[:, :, None, None, None, None]
    tz = k5[jnp.clip(jj - ii, 0, S5_STEP - 1)] * causal
    w3u = tz.transpose(3, 0, 4, 5, 1, 2)
    w3u = (w3u[:, :, :, :, :, None, :] * eye[None, None, :, None, None, :, None]).reshape(np_, MXU_TILE, MXU_TILE)
    w3 = jnp.concatenate([w3x, w3u], axis=1)

    pr, pi = pwr.T, pwi.T
    pos = jnp.arange(SUBLANES)[:, None]
    tabs = []
    for k in (1, 2, 4):
        keep = (pos >= k).astype(F32)
        tabs += [keep * pr[k - 1][None], keep * pi[k - 1][None]]
    tabs += [pr, pi]
    return w1.astype(BF16), w3.astype(BF16), jnp.stack(tabs, axis=0)


def _lane_window(lo):
    lane = lax.broadcasted_iota(jnp.int32, (1, LANES), 1)
    return (lane >= lo) & (lane < lo + PAIR_CH)


def _merge_windows(pieces):
    acc = pieces[0]
    for k in range(1, len(pieces)):
        acc = jnp.where(_lane_window(PAIR_CH * k), pieces[k], acc)
    return acc


def _rolled(cache, key, make, shift):
    if (key, shift) not in cache:
        x = make()
        cache[(key, shift)] = x if shift == 0 else pltpu.roll(x, shift, axis=1)
    return cache[(key, shift)]


def _s5_regroup_in(u8):
    per_tile = LANES // PAIR_CH
    cache, outs = {}, []
    for q in range(S5_PAIRS):
        halves = []
        for m in range(MXU_TILE // LANES):
            pieces = []
            for i in range(per_tile * m, per_tile * (m + 1)):
                src = per_tile * i + q // per_tile
                pieces.append(_rolled(cache, src, lambda s=src: u8[:, LANES * s:LANES * (s + 1)],
                                      PAIR_CH * ((i - q) % per_tile)))
            halves.append(_merge_windows(pieces))
        outs.append(jnp.concatenate(halves, axis=1))
    return outs


def _s5_regroup_out(ys):
    per_tile = LANES // PAIR_CH
    cache, tiles = {}, []
    for j in range(S5_STEP):
        for qq in range(S5_PAIRS // per_tile):
            pieces = []
            for q in range(per_tile * qq, per_tile * (qq + 1)):
                m = j // per_tile
                pieces.append(_rolled(cache, (q, m), lambda q=q, m=m: ys[q][:, LANES * m:LANES * (m + 1)],
                                      PAIR_CH * ((q - j) % per_tile)))
            tiles.append(_merge_windows(pieces))
    return jnp.concatenate(tiles, axis=1)


def _mix_kernel(q_ref, k_ref, la_ref, v_ref, r_ref, u8_ref, st0_ref, s50_ref, gh_ref,
                w1_ref, w3_ref, d8_ref, tab_ref,
                og_ref, g5_ref, sto_ref, s5o_ref,
                st_sc, car_sc, o_sc, *, tb, unroll, streams):
    nm = S5_MODES

    if not streams:
        @pl.when(pl.program_id(1) == 0)
        def _():
            st_sc[...] = st0_ref[0]
            car_sc[...] = jnp.broadcast_to(s50_ref[0], (SUBLANES, 2 * nm))

    ti = lax.broadcasted_iota(jnp.int32, (CHUNK, CHUNK), 0)
    si = lax.broadcasted_iota(jnp.int32, (CHUNK, CHUNK), 1)
    causal = si <= ti
    tril = jnp.where(causal, 1.0, 0.0).astype(BF16)
    lane = lax.broadcasted_iota(jnp.int32, (1, GLA_KEY), 1) % LANES
    head_mask = (lane < GLA_DK, lane >= GLA_DK)
    nsub = CHUNK // SUB

    def chunk(c):
        rows = pl.ds(c * CHUNK if isinstance(c, int) else pl.multiple_of(c * CHUNK, CHUNK), CHUNK)
        la = la_ref[rows, :]
        hi = la.astype(BF16)
        lo = (la - hi.astype(F32)).astype(BF16)
        b = _dot(tril, hi) + _dot(tril, lo)
        refs = [jnp.zeros((1, GLA_KEY), F32)] + [b[SUB * i - 1:SUB * i, :] for i in range(1, nsub)]
        rb = jnp.concatenate([jnp.broadcast_to(x, (SUB, GLA_KEY)) for x in refs], axis=0)
        q, k = q_ref[rows, :], k_ref[rows, :]
        b_last = b[CHUNK - 1:CHUNK, :]
        qt = q * jnp.exp(b - rb)
        qd = q * jnp.exp(b)
        qt = [jnp.where(m, qt, 0.0).astype(BF16) for m in head_mask]
        qd = [jnp.where(m, qd, 0.0).astype(BF16) for m in head_mask]
        kd = (k * jnp.exp(b_last - b)).astype(BF16)
        e_last = jnp.exp(b_last)
        kts = [(k * jnp.exp(jnp.minimum(x - b, EXP_CLAMP))).astype(BF16) for x in refs]
        for h in range(GLA_HEADS):
            pc = slice(LANES * (h // 2), LANES * (h // 2 + 1))
            qt_h = qt[h % 2][:, pc]
            sc = jnp.concatenate(
                [_dot_nt(qt_h[SUB * i:SUB * (i + 1)], kts[i][:, pc]) for i in range(nsub)], axis=0)
            pm = jnp.where(causal, sc, 0.0).astype(BF16)
            vh = v_ref[rows, GLA_DV * h:GLA_DV * (h + 1)]
            st = st0_ref[0, c * GLA_HEADS + h] if streams else st_sc[h]
            o = _dot(pm, vh) + _dot_nt(qd[h % 2][:, pc], st.astype(BF16))
            o_sc[rows, GLA_DV * h:GLA_DV * (h + 1)] = o
            vt = vh.astype(F32).T.astype(BF16)
            st_new = st * e_last[:, pc] + _dot(vt, kd[:, pc])
            if streams:
                sto_ref[0, c * GLA_HEADS + h] = st_new
            else:
                st_sc[h] = st_new

    nchunks = tb // CHUNK
    if nchunks <= unroll:
        for c in range(nchunks):
            chunk(c)
    else:
        def body(i, carry):
            for j in range(unroll):
                chunk(i * unroll + j)
            return carry
        lax.fori_loop(0, nchunks // unroll, body, 0)

    o_all = o_sc[...]
    normed = []
    for h in range(GLA_HEADS):
        oh = o_all[:, GLA_DV * h:GLA_DV * (h + 1)]
        normed.append(oh * lax.rsqrt(jnp.mean(oh * oh, axis=-1, keepdims=True) + EPS))
    r = r_ref[...]
    og = jnp.concatenate(normed, axis=1) * gh_ref[...] * (r * _sigmoid(r))
    og_ref[...] = og.astype(BF16)
    if not streams:
        sto_ref[0] = st_sc[...]

    ng = tb // S5_STEP // SUBLANES
    u8 = u8_ref[...]
    uq = [x.astype(BF16) for x in _s5_regroup_in(u8)]
    w = [_dot(uq[q], w1_ref[q]) for q in range(S5_PAIRS)]
    xr = jnp.concatenate([x[:, :LANES] for x in w], axis=1).reshape(ng, SUBLANES, nm)
    xi = jnp.concatenate([x[:, LANES:] for x in w], axis=1).reshape(ng, SUBLANES, nm)
    for s, kstep in enumerate((1, 2, 4)):
        ar, ai = tab_ref[2 * s], tab_ref[2 * s + 1]
        sr = pltpu.roll(xr, kstep, axis=1)
        sim = pltpu.roll(xi, kstep, axis=1)
        xr, xi = xr + ar * sr - ai * sim, xi + ar * sim + ai * sr
    pr, pi = tab_ref[6], tab_ref[7]
    first_row = lax.broadcasted_iota(jnp.int32, (SUBLANES, nm), 0) == 0
    if not streams:
        cr, ci = car_sc[:, :nm], car_sc[:, nm:]
    prev_r, prev_i = [], []
    for g in range(ng):
        if streams:
            cr = jnp.broadcast_to(s50_ref[0, g:g + 1, :nm], (SUBLANES, nm))
            ci = jnp.broadcast_to(s50_ref[0, g:g + 1, nm:], (SUBLANES, nm))
        yr = xr[g] + pr * cr - pi * ci
        yi = xi[g] + pr * ci + pi * cr
        prev_r.append(jnp.where(first_row, cr, pltpu.roll(yr, 1, axis=0)))
        prev_i.append(jnp.where(first_row, ci, pltpu.roll(yi, 1, axis=0)))
        if streams:
            s5o_ref[0, g:g + 1, :nm] = yr[SUBLANES - 1:SUBLANES]
            s5o_ref[0, g:g + 1, nm:] = yi[SUBLANES - 1:SUBLANES]
        else:
            cr = jnp.broadcast_to(yr[SUBLANES - 1:SUBLANES], (SUBLANES, nm))
            ci = jnp.broadcast_to(yi[SUBLANES - 1:SUBLANES], (SUBLANES, nm))
    if not streams:
        car_sc[:, :nm] = cr
        car_sc[:, nm:] = ci
        s5o_ref[0] = car_sc[0:1, :]
    prev_r = jnp.concatenate(prev_r, axis=0).astype(BF16)
    prev_i = jnp.concatenate(prev_i, axis=0).astype(BF16)
    ys = []
    for q in range(S5_PAIRS):
        ms = slice(LANES * q, LANES * (q + 1))
        ys.append(_dot(jnp.concatenate([prev_r[:, ms], prev_i[:, ms], uq[q]], axis=1), w3_ref[q]))
    y8 = _s5_regroup_out(ys) + d8_ref[...] * u8
    inner = math.sqrt(2.0 / math.pi) * (y8 + 0.044715 * (y8 * y8 * y8))
    g5_ref[...] = (0.5 * y8 * (1.0 + jnp.tanh(inner))).astype(BF16)


def _mix(q, k, la, v, r, u5, st0, s50, p, nb, tb, unroll, streams):
    n = q.shape[0]
    nt = n // (nb * tb)
    nst = st0.shape[1]
    nseq = s50.shape[1]
    tg = tb // S5_STEP
    row = lambda w: pl.BlockSpec((tb, w), lambda b, t: (b * nt + t, 0))
    row8 = pl.BlockSpec((tg, S5_ROW), lambda b, t: (b * nt + t, 0))
    st_spec = pl.BlockSpec((1, nst, GLA_DV, LANES), lambda b, t: (b, 0, 0, 0))
    s5_spec = pl.BlockSpec((1, nseq, 2 * S5_MODES), lambda b, t: (b, 0, 0))
    in_specs = [row(GLA_KEY), row(GLA_KEY), row(GLA_KEY), row(GLA_VAL), row(GLA_VAL), row8,
                st_spec, s5_spec, _const_spec((1, GLA_VAL)),
                _const_spec((S5_PAIRS, MXU_TILE, MXU_TILE)), _const_spec((S5_PAIRS, 2 * MXU_TILE, MXU_TILE)),
                _const_spec((1, S5_ROW)), _const_spec((N_SCAN_TABLES, SUBLANES, S5_MODES))]
    out_shape = [jax.ShapeDtypeStruct((n, GLA_VAL), BF16), jax.ShapeDtypeStruct((n // S5_STEP, S5_ROW), BF16),
                 jax.ShapeDtypeStruct(st0.shape, F32), jax.ShapeDtypeStruct(s50.shape, F32)]
    return pl.pallas_call(
        functools.partial(_mix_kernel, tb=tb, unroll=unroll, streams=streams),
        out_shape=out_shape,
        grid=(nb, nt),
        in_specs=in_specs,
        out_specs=[row(GLA_VAL), row8, st_spec, s5_spec],
        scratch_shapes=[pltpu.VMEM((GLA_HEADS, GLA_DV, LANES), F32),
                        pltpu.VMEM((SUBLANES, 2 * S5_MODES), F32),
                        pltpu.VMEM((tb, GLA_VAL), F32)],
        compiler_params=pltpu.CompilerParams(dimension_semantics=("arbitrary", "arbitrary"),
                                             vmem_limit_bytes=VMEM_LIMIT),
        name="mix",
    )(q, k, la, v, r, u5.reshape(n // S5_STEP, S5_ROW), st0, s50, p['g_gla_head'], p['w1'], p['w3'], p['d8'], p['tab'])


def _back_kernel(h_ref, og_ref, g5_ref, gmg_ref, gms_ref, wgo_ref, wa_ref, wb_ref, wso_ref, wo_ref,
                 g2_ref, wg_ref, wu_ref, wd_ref, gf_ref, y_ref):
    y_gla = _dot(og_ref[...], wgo_ref[...])
    g5 = g5_ref[...]
    glu = _dot(g5, wa_ref[...]) * _sigmoid(_dot(g5, wb_ref[...]))
    y_s5 = _dot(glu.astype(BF16), wso_ref[...])
    m = _sigmoid(gmg_ref[...]) * y_gla + _sigmoid(gms_ref[...]) * y_s5
    h = h_ref[...] + _dot(m.astype(BF16), wo_ref[...])
    hn = _rms(h, g2_ref[...]).astype(BF16)
    h = h + 0.5 * _swiglu(hn, wg_ref, wu_ref, wd_ref)
    y_ref[...] = _rms(h, gf_ref[...])


def _back(h, og, g5, gmg, gms, p, tm):
    n = h.shape[0]
    row = lambda w: pl.BlockSpec((tm, w), lambda i: (i, 0))
    in_specs = [row(D_MODEL), row(GLA_VAL), row(S5_WIDTH), row(D_MODEL), row(D_MODEL),
                _const_spec((GLA_VAL, D_MODEL)), _const_spec((S5_WIDTH, S5_WIDTH)), _const_spec((S5_WIDTH, S5_WIDTH)),
                _const_spec((S5_WIDTH, D_MODEL)), _const_spec((D_MODEL, D_MODEL)), _const_spec((1, D_MODEL)),
                _const_spec((D_MODEL, D_FF)), _const_spec((D_MODEL, D_FF)), _const_spec((D_FF, D_MODEL)),
                _const_spec((1, D_MODEL))]
    return pl.pallas_call(
        _back_kernel,
        out_shape=jax.ShapeDtypeStruct((n, D_MODEL), F32),
        grid=(n // tm,),
        in_specs=in_specs,
        out_specs=row(D_MODEL),
        compiler_params=pltpu.CompilerParams(dimension_semantics=("parallel",), vmem_limit_bytes=VMEM_LIMIT),
        name="back",
    )(h, og, g5, gmg, gms, p['w_gla_out'], p['w_glu_a'], p['w_glu_b'], p['w_s5_out'], p['w_out'],
      p['g_ffn2'], p['wg2'], p['wu2'], p['wd2'], p['g_final'])


def _gla_state_in(s, nb):
    st = jnp.swapaxes(s.astype(F32), 2, 3)
    z = jnp.zeros_like(st)
    sel = (jnp.arange(GLA_HEADS) % 2 == 0)[None, :, None, None]
    st = jnp.where(sel, jnp.concatenate([st, z], axis=-1), jnp.concatenate([z, st], axis=-1))
    return st.reshape(nb, -1, GLA_DV, LANES)


def _gla_state_out(st, bsz):
    st = st.reshape(bsz, GLA_HEADS, GLA_DV, LANES)
    sel = (jnp.arange(GLA_HEADS) % 2 == 0)[None, :, None, None]
    return jnp.swapaxes(jnp.where(sel, st[..., :GLA_DK], st[..., GLA_DK:]), 2, 3)


def _s5_state_in(x0, nb):
    x = x0.astype(F32).reshape(x0.shape[0], S5_MODES, 2)
    return jnp.concatenate([x[..., 0], x[..., 1]], axis=-1).reshape(nb, -1, 2 * S5_MODES)


def _s5_state_out(x, bsz):
    x = x.reshape(bsz, 2, S5_MODES)
    return jnp.stack([x[:, 0], x[:, 1]], axis=-1).reshape(bsz, S5_GROUPS, S5_STATE, 2)


def _layer(x, s_gla, s_s5, p, tm, tb, unroll, streams):
    bsz, t, _ = x.shape
    n = bsz * t
    nb = n // tb if streams else bsz
    h, q, k, v, r, u5, gmg, gms, la = _front(x.reshape(n, D_MODEL), p, tm)
    og, g5, st, s5 = _mix(q, k, la, v, r, u5, _gla_state_in(s_gla, nb), _s5_state_in(s_s5, nb), p,
                          nb, tb, unroll, streams)
    y = _back(h, og, g5.reshape(n, S5_WIDTH), gmg, gms, p, tm)
    return y.reshape(bsz, t, D_MODEL), _gla_state_out(st, bsz), _s5_state_out(s5, bsz)


def kernel(x_prompt, x_sample, state_gla, state_s5, g_ffn1, w_ffn1_gate, w_ffn1_up, w_ffn1_down, g_mix, w_in, w_gate_up, b_gate, g_gla_head, w_gla_out, s5_lam_re, s5_lam_im, s5_log_dt, s5_b_re, s5_b_im, s5_c_re, s5_c_im, s5_d, w_glu_a, w_glu_b, w_s5_out, w_out, g_ffn2, w_ffn2_gate, w_ffn2_up, w_ffn2_down, g_final):
    vec = lambda a: a.astype(F32).reshape(1, -1)
    sizes = (GLA_KEY, GLA_KEY, GLA_VAL, GLA_VAL, GLA_GATE_RANK, S5_WIDTH, D_MODEL, D_MODEL)
    offs = [sum(sizes[:i]) for i in range(len(sizes) + 1)]
    wq, wk, wv, wr, wga, wu5, wgg, wgs = [w_in[:, offs[i]:offs[i + 1]] for i in range(len(sizes))]
    w1, w3, tab = _s5_tables(*_s5_prep(s5_lam_re, s5_lam_im, s5_log_dt, s5_b_re, s5_b_im, s5_c_re, s5_c_im))
    p = {
        'g_ffn1': vec(g_ffn1), 'wg1': w_ffn1_gate.astype(BF16), 'wu1': w_ffn1_up.astype(BF16),
        'wd1': w_ffn1_down.astype(BF16), 'g_mix': vec(g_mix),
        'w_main': jnp.concatenate([wq, wk, wv, wr, wu5, wgg, wgs], axis=1).astype(BF16),
        'w_ga': jnp.pad(wga, ((0, 0), (0, LANES - GLA_GATE_RANK))).astype(BF16),
        'w_gu': jnp.pad(w_gate_up, ((0, LANES - GLA_GATE_RANK), (0, 0))).astype(BF16),
        'b_gate': vec(b_gate), 'g_gla_head': vec(g_gla_head),
        'w1': w1, 'w3': w3, 'tab': tab, 'd8': jnp.tile(vec(s5_d), (1, S5_STEP)),
        'w_gla_out': w_gla_out.astype(BF16), 'w_glu_a': w_glu_a.astype(BF16), 'w_glu_b': w_glu_b.astype(BF16),
        'w_s5_out': w_s5_out.astype(BF16), 'w_out': w_out.astype(BF16),
        'g_ffn2': vec(g_ffn2), 'wg2': w_ffn2_gate.astype(BF16), 'wu2': w_ffn2_up.astype(BF16),
        'wd2': w_ffn2_down.astype(BF16), 'g_final': vec(g_final),
    }
    bp = x_prompt.shape[0]
    zero_gla = jnp.zeros((bp, GLA_HEADS, GLA_DK, GLA_DV), F32)
    zero_s5 = jnp.zeros((bp, S5_GROUPS, S5_STATE, 2), F32)
    y_p, gla_p, s5_p = _layer(x_prompt, zero_gla, zero_s5, p, tm=256, tb=1024, unroll=4, streams=False)
    n_sample = x_sample.shape[0] * x_sample.shape[1]
    y_s, gla_s, s5_s = _layer(x_sample, state_gla, state_s5, p, tm=256, tb=n_sample, unroll=2, streams=True)
    return (y_p, y_s, gla_p, s5_p, gla_s, s5_s)
```

```python
import functools
import math

import jax
import jax.numpy as jnp
from jax import lax
from jax.experimental import pallas as pl
from jax.experimental.pallas import tpu as pltpu

F32 = jnp.float32
BF16 = jnp.bfloat16

D_MODEL = 1024
D_FF = 2816
CHUNK = 64
SUB = 16
GLA_HEADS = 4
GLA_DK = 64
GLA_DV = 128
GLA_KEY = GLA_HEADS * GLA_DK
GLA_VAL = GLA_HEADS * GLA_DV
GLA_GATE_RANK = 16
GLA_GATE_TAU = 16.0
S5_WIDTH = 512
S5_GROUP = 16
S5_GROUPS = 32
S5_STATE = 64
S5_MODES = S5_GROUPS * S5_STATE
EPS = 1e-6
SUBLANES = 8
LANES = 128
MXU_TILE = 256
S5_STEP = SUBLANES
S5_PAIRS = S5_GROUPS // 2
PAIR_CH = 2 * S5_GROUP
PAIR_MODES = 2 * S5_STATE
S5_SLABS = S5_WIDTH // LANES
N_SCAN_TABLES = 8
EXP_CLAMP = 80.0
FF_CHUNK = 512
VMEM_LIMIT = 60 * 1024 * 1024

_OQ, _OK, _OV, _OR, _OU, _OGG, _OGS = 0, 256, 512, 1024, 1536, 2048, 3072
W_MAIN_COLS = 4096


def _dot(a, b):
    return jnp.dot(a, b, preferred_element_type=F32)


def _dot_nt(a, b):
    return lax.dot_general(a, b, (((1,), (1,)), ((), ())), preferred_element_type=F32)


def _split_dot(a, b):
    a1 = a.astype(BF16)
    a2 = (a - a1.astype(F32)).astype(BF16)
    b1 = b.astype(BF16)
    b2 = (b - b1.astype(F32)).astype(BF16)
    return _dot(a1, b1) + _dot(a1, b2) + _dot(a2, b1)


def _cmul(ar, ai, br, bi):
    return ar * br - ai * bi, ar * bi + ai * br


def _sigmoid(x):
    return 1.0 / (1.0 + jnp.exp(-x))


def _rms(x, g):
    ms = jnp.mean(x * x, axis=-1, keepdims=True)
    return x * lax.rsqrt(ms + EPS) * g


def _ff_chunks():
    out, c0 = [], 0
    while c0 < D_FF:
        n = min(FF_CHUNK, D_FF - c0)
        out.append((c0, n))
        c0 += n
    return out


def _swiglu(xn, wg_ref, wu_ref, wd_ref):
    acc = None
    for c0, n in _ff_chunks():
        g = _dot(xn, wg_ref[:, c0:c0 + n])
        u = _dot(xn, wu_ref[:, c0:c0 + n])
        a = (g * _sigmoid(g) * u).astype(BF16)
        part = _dot(a, wd_ref[c0:c0 + n, :])
        acc = part if acc is None else acc + part
    return acc


def _const_spec(shape):
    nd = len(shape)
    return pl.BlockSpec(shape, lambda *_: (0,) * nd, pipeline_mode=pl.Buffered(1))


def _front_kernel(x_ref, g1_ref, wg_ref, wu_ref, wd_ref, gmix_ref, win_ref, wga_ref, wgu_ref, bg_ref,
                  h_ref, q_ref, k_ref, v_ref, r_ref, u5_ref, gmg_ref, gms_ref, la_ref):
    x = x_ref[...]
    xn = _rms(x, g1_ref[...]).astype(BF16)
    h = x + 0.5 * _swiglu(xn, wg_ref, wu_ref, wd_ref)
    h_ref[...] = h
    un = _rms(h, gmix_ref[...]).astype(BF16)

    def proj(c0, n):
        return _dot(un, win_ref[:, c0:c0 + n])

    q_ref[...] = proj(_OQ, GLA_KEY) * (GLA_DK ** -0.5)
    k_ref[...] = proj(_OK, GLA_KEY)
    v_ref[...] = proj(_OV, GLA_VAL).astype(BF16)
    r_ref[...] = proj(_OR, GLA_VAL)
    u5 = proj(_OU, S5_WIDTH)
    for c in range(S5_SLABS):
        u5_ref[c] = u5[:, LANES * c:LANES * (c + 1)]
    gmg_ref[...] = proj(_OGG, D_MODEL)
    gms_ref[...] = proj(_OGS, D_MODEL)
    ga = _dot(un, wga_ref[...])
    pre = _dot(ga.astype(BF16), wgu_ref[...]) + bg_ref[...]
    log_sig = jnp.minimum(pre, 0.0) - jnp.log(1.0 + jnp.exp(-jnp.abs(pre)))
    la_ref[...] = log_sig * (1.0 / GLA_GATE_TAU)


def _front(x2d, p, tm):
    n = x2d.shape[0]
    row = lambda w: pl.BlockSpec((tm, w), lambda i: (i, 0))
    slab = pl.BlockSpec((S5_SLABS, tm, LANES), lambda i: (0, i, 0))
    in_specs = [row(D_MODEL), _const_spec((1, D_MODEL)),
                _const_spec((D_MODEL, D_FF)), _const_spec((D_MODEL, D_FF)), _const_spec((D_FF, D_MODEL)),
                _const_spec((1, D_MODEL)), _const_spec((D_MODEL, W_MAIN_COLS)),
                _const_spec((D_MODEL, LANES)), _const_spec((LANES, GLA_KEY)), _const_spec((1, GLA_KEY))]
    sds = lambda w, dt: jax.ShapeDtypeStruct((n, w), dt)
    out_shape = [sds(D_MODEL, F32), sds(GLA_KEY, F32), sds(GLA_KEY, F32), sds(GLA_VAL, BF16), sds(GLA_VAL, F32),
                 jax.ShapeDtypeStruct((S5_SLABS, n, LANES), F32), sds(D_MODEL, F32), sds(D_MODEL, F32),
                 sds(GLA_KEY, F32)]
    out_specs = [row(D_MODEL), row(GLA_KEY), row(GLA_KEY), row(GLA_VAL), row(GLA_VAL), slab,
                 row(D_MODEL), row(D_MODEL), row(GLA_KEY)]
    return pl.pallas_call(
        _front_kernel,
        out_shape=out_shape,
        grid=(n // tm,),
        in_specs=in_specs,
        out_specs=out_specs,
        compiler_params=pltpu.CompilerParams(dimension_semantics=("parallel",), vmem_limit_bytes=VMEM_LIMIT),
        name="front",
    )(x2d, p['g_ffn1'], p['wg1'], p['wu1'], p['wd1'], p['g_mix'], p['w_main'], p['w_ga'], p['w_gu'], p['b_gate'])


def _shift_lanes_256(x, s):
    lane = lax.broadcasted_iota(jnp.int32, (1, LANES), 1)
    lo, hi = x[:, :LANES], x[:, LANES:]
    if s == 0:
        return x
    if s >= LANES:
        r = s - LANES
        moved = lo if r == 0 else jnp.where(lane >= r, pltpu.roll(lo, r, axis=1), 0.0)
        return jnp.concatenate([jnp.zeros_like(lo), moved], axis=1)
    rl, rh = pltpu.roll(lo, s, axis=1), pltpu.roll(hi, s, axis=1)
    return jnp.concatenate([jnp.where(lane >= s, rl, 0.0), jnp.where(lane >= s, rh, rl)], axis=1)


def _s5_prep_kernel(lrc_ref, lic_ref, ldtc_ref, lrr_ref, lir_ref, ldtr_ref, br_ref, bi_ref, cr_ref, ci_ref,
                    w1r_ref, w1i_ref, w3xr_ref, w3xi_ref, w3u_ref, tab_ref):
    lr, li = lrc_ref[...], lic_ref[...]
    dt = jnp.exp(ldtc_ref[...])
    ar, th = lr * dt, li * dt

    def powers(kvec):
        mag = jnp.exp(ar * kvec)
        ang = th * kvec
        return mag * jnp.cos(ang), mag * jnp.sin(ang)

    lane = lax.broadcasted_iota(jnp.int32, (1, MXU_TILE), 1)
    step = lax.shift_right_logical(lane, 5)
    lane_group = lax.shift_right_logical(lane, 4) & 1
    mode = lax.broadcasted_iota(jnp.int32, (S5_MODES, 1), 0)
    own = lane_group == (lax.shift_right_logical(mode, 6) & 1)
    chan = lax.broadcasted_iota(jnp.int32, (S5_GROUP, 1), 0)
    rep = jnp.where((lane & (S5_GROUP - 1)) == chan, 1.0, 0.0)
    b_r, b_i = _split_dot(br_ref[...], rep), _split_dot(bi_ref[...], rep)
    c_r, c_i = _split_dot(cr_ref[...], rep), _split_dot(ci_ref[...], rep)

    ab_re, ab_im = powers(jnp.ones((1, 1), F32))
    nr, ni = ab_re - 1.0, ab_im
    den = lr * lr + li * li
    f_re = (nr * lr + ni * li) / den
    f_im = (ni * lr - nr * li) / den
    bbr, bbi = _cmul(f_re, f_im, b_r, b_i)

    p7r, p7i = powers((S5_STEP - 1 - step).astype(F32))
    w1r, w1i = _cmul(p7r, p7i, bbr, bbi)
    w1r_ref[...] = jnp.where(own, w1r, 0.0).astype(BF16)
    w1i_ref[...] = jnp.where(own, w1i, 0.0).astype(BF16)
    p1r, p1i = powers((step + 1).astype(F32))
    xr, xi = _cmul(p1r, p1i, c_r, c_i)
    w3xr_ref[...] = jnp.where(own, xr, 0.0).astype(BF16)
    w3xi_ref[...] = jnp.where(own, -xi, 0.0).astype(BF16)

    p0r, p0i = powers(step.astype(F32))
    car, cai = _cmul(p0r, p0i, c_r, c_i)
    car, cai = jnp.where(own, car, 0.0), jnp.where(own, cai, 0.0)
    lane_w = lax.broadcasted_iota(jnp.int32, (1, S5_WIDTH), 1)
    rep_w = jnp.where((lane_w & (S5_GROUP - 1)) == chan, 1.0, 0.0)
    own_w = lax.shift_right_logical(lane_w, 4) == lax.shift_right_logical(mode, 6)
    dbr, dbi = _cmul(f_re, f_im, _split_dot(br_ref[...], rep_w), _split_dot(bi_ref[...], rep_w))
    dbr, dbi = jnp.where(own_w, dbr, 0.0), jnp.where(own_w, dbi, 0.0)
    taps = _split_dot(dbr.T, car) - _split_dot(dbi.T, cai)
    for i in range(S5_STEP):
        blk = _shift_lanes_256(taps, PAIR_CH * i)
        w3u_ref[:, PAIR_CH * i:PAIR_CH * (i + 1), :] = blk.reshape(S5_PAIRS, PAIR_CH, MXU_TILE).astype(BF16)

    arr = lrr_ref[...] * jnp.exp(ldtr_ref[...])
    thr = lir_ref[...] * jnp.exp(ldtr_ref[...])
    pos = lax.broadcasted_iota(jnp.int32, (SUBLANES, 1), 0)

    def powers_rows(kcol):
        mag = jnp.exp(arr * kcol)
        ang = thr * kcol
        return mag * jnp.cos(ang), mag * jnp.sin(ang)

    for s, kstep in enumerate((1, 2, 4)):
        kr, ki = powers_rows(jnp.full((1, 1), float(S5_STEP * kstep), F32))
        tab_ref[2 * s] = jnp.where(pos >= kstep, kr, 0.0)
        tab_ref[2 * s + 1] = jnp.where(pos >= kstep, ki, 0.0)
    tab_ref[6], tab_ref[7] = powers_rows(((pos + 1) * S5_STEP).astype(F32))


def _s5_prep(lam_re, lam_im, log_dt, b_re, b_im, c_re, c_im):
    m = S5_MODES
    ldt = jnp.broadcast_to(log_dt.astype(F32)[:, None], (S5_GROUPS, S5_STATE))
    col = lambda a: a.astype(F32).reshape(m, 1)
    rowv = lambda a: a.astype(F32).reshape(1, m)
    bmat = lambda a: a.astype(F32).reshape(m, S5_GROUP)
    cmat = lambda a: a.astype(F32).transpose(0, 2, 1).reshape(m, S5_GROUP)
    tile = jax.ShapeDtypeStruct((m, MXU_TILE), BF16)
    out_shape = [tile] * 4 + [jax.ShapeDtypeStruct((S5_PAIRS, MXU_TILE, MXU_TILE), BF16),
                              jax.ShapeDtypeStruct((N_SCAN_TABLES, SUBLANES, m), F32)]
    w1r, w1i, w3xr, w3xi, w3u, tab = pl.pallas_call(
        _s5_prep_kernel,
        out_shape=out_shape,
        compiler_params=pltpu.CompilerParams(vmem_limit_bytes=VMEM_LIMIT),
        name="s5_prep",
    )(col(lam_re), col(lam_im), col(ldt), rowv(lam_re), rowv(lam_im), rowv(ldt),
      bmat(b_re), bmat(b_im), cmat(c_re), cmat(c_im))
    per_pair = lambda a: a.reshape(S5_PAIRS, PAIR_MODES, MXU_TILE)
    return per_pair(w1r), per_pair(w1i), per_pair(w3xr), per_pair(w3xi), w3u, tab


def _lane_window(lo):
    lane = lax.broadcasted_iota(jnp.int32, (1, LANES), 1)
    return (lane >= lo) & (lane < lo + PAIR_CH)


def _merge_windows(pieces):
    acc = pieces[0]
    for k in range(1, len(pieces)):
        acc = jnp.where(_lane_window(PAIR_CH * k), pieces[k], acc)
    return acc


def _rolled(cache, key, make, shift):
    if (key, shift) not in cache:
        x = make()
        cache[(key, shift)] = x if shift == 0 else pltpu.roll(x, shift, axis=1)
    return cache[(key, shift)]


def _s5_regroup_in(step_slab):
    per_tile = LANES // PAIR_CH
    cache, outs = {}, []
    for q in range(S5_PAIRS):
        halves = []
        for m in range(MXU_TILE // LANES):
            pieces = []
            for i in range(per_tile * m, per_tile * (m + 1)):
                c = q // per_tile
                pieces.append(_rolled(cache, (i, c), lambda i=i, c=c: step_slab(i, c),
                                      PAIR_CH * ((i - q) % per_tile)))
            halves.append(_merge_windows(pieces))
        outs.append(jnp.concatenate(halves, axis=1))
    return outs


def _s5_regroup_out(ys):
    per_tile = LANES // PAIR_CH
    cache, out = {}, []
    for j in range(S5_STEP):
        slabs = []
        for c in range(S5_SLABS):
            pieces = []
            for q in range(per_tile * c, per_tile * (c + 1)):
                m = j // per_tile
                pieces.append(_rolled(cache, (q, m), lambda q=q, m=m: ys[q][:, LANES * m:LANES * (m + 1)],
                                      PAIR_CH * ((q - j) % per_tile)))
            slabs.append(_merge_windows(pieces))
        out.append(slabs)
    return out


def _mix_kernel(q_ref, k_ref, la_ref, v_ref, r_ref, u5_ref, st0_ref, s50_ref, gh_ref,
                w1r_ref, w1i_ref, w3xr_ref, w3xi_ref, w3u_ref, d_ref, tab_ref,
                og_ref, g5_ref, sto_ref, s5o_ref,
                st_sc, car_sc, o_sc, g5_sc, *, tb, unroll, streams):
    nm = S5_MODES

    if not streams:
        @pl.when(pl.program_id(1) == 0)
        def _():
            st_sc[...] = st0_ref[0]
            car_sc[...] = jnp.broadcast_to(s50_ref[0], (SUBLANES, 2 * nm))

    ti = lax.broadcasted_iota(jnp.int32, (CHUNK, CHUNK), 0)
    si = lax.broadcasted_iota(jnp.int32, (CHUNK, CHUNK), 1)
    causal = si <= ti
    tril = jnp.where(causal, 1.0, 0.0).astype(BF16)
    lane = lax.broadcasted_iota(jnp.int32, (1, GLA_KEY), 1) % LANES
    head_mask = (lane < GLA_DK, lane >= GLA_DK)
    nsub = CHUNK // SUB

    def chunk(c):
        rows = pl.ds(c * CHUNK if isinstance(c, int) else pl.multiple_of(c * CHUNK, CHUNK), CHUNK)
        la = la_ref[rows, :]
        hi = la.astype(BF16)
        lo = (la - hi.astype(F32)).astype(BF16)
        b = _dot(tril, hi) + _dot(tril, lo)
        refs = [jnp.zeros((1, GLA_KEY), F32)] + [b[SUB * i - 1:SUB * i, :] for i in range(1, nsub)]
        rb = jnp.concatenate([jnp.broadcast_to(x, (SUB, GLA_KEY)) for x in refs], axis=0)
        q, k = q_ref[rows, :], k_ref[rows, :]
        b_last = b[CHUNK - 1:CHUNK, :]
        qt = q * jnp.exp(b - rb)
        qd = q * jnp.exp(b)
        qt = [jnp.where(m, qt, 0.0).astype(BF16) for m in head_mask]
        qd = [jnp.where(m, qd, 0.0).astype(BF16) for m in head_mask]
        kd = (k * jnp.exp(b_last - b)).astype(BF16)
        e_last = jnp.exp(b_last)
        kts = [(k * jnp.exp(jnp.minimum(x - b, EXP_CLAMP))).astype(BF16) for x in refs]
        for h in range(GLA_HEADS):
            pc = slice(LANES * (h // 2), LANES * (h // 2 + 1))
            qt_h = qt[h % 2][:, pc]
            sc = jnp.concatenate(
                [_dot_nt(qt_h[SUB * i:SUB * (i + 1)], kts[i][:, pc]) for i in range(nsub)], axis=0)
            pm = jnp.where(causal, sc, 0.0).astype(BF16)
            vh = v_ref[rows, GLA_DV * h:GLA_DV * (h + 1)]
            st = st0_ref[0, c * GLA_HEADS + h] if streams else st_sc[h]
            o = _dot(pm, vh) + _dot_nt(qd[h % 2][:, pc], st.astype(BF16))
            o_sc[rows, GLA_DV * h:GLA_DV * (h + 1)] = o
            vt = vh.astype(F32).T.astype(BF16)
            st_new = st * e_last[:, pc] + _dot(vt, kd[:, pc])
            if streams:
                sto_ref[0, c * GLA_HEADS + h] = st_new
            else:
                st_sc[h] = st_new

    nchunks = tb // CHUNK
    if nchunks <= unroll:
        for c in range(nchunks):
            chunk(c)
    else:
        def body(i, carry):
            for j in range(unroll):
                chunk(i * unroll + j)
            return carry
        lax.fori_loop(0, nchunks // unroll, body, 0)

    o_all = o_sc[...]
    normed = []
    for h in range(GLA_HEADS):
        oh = o_all[:, GLA_DV * h:GLA_DV * (h + 1)]
        normed.append(oh * lax.rsqrt(jnp.mean(oh * oh, axis=-1, keepdims=True) + EPS))
    r = r_ref[...]
    og = jnp.concatenate(normed, axis=1) * gh_ref[...] * (r * _sigmoid(r))
    og_ref[...] = og.astype(BF16)
    if not streams:
        sto_ref[0] = st_sc[...]

    tg = tb // S5_STEP
    ng = tg // SUBLANES
    step_slab = lambda i, c: u5_ref[c, pl.ds(i, tg, stride=S5_STEP), :]
    uq = [x.astype(BF16) for x in _s5_regroup_in(step_slab)]
    w = [_dot_nt(uq[q], jnp.concatenate([w1r_ref[q], w1i_ref[q]], axis=0)) for q in range(S5_PAIRS)]
    xr = jnp.concatenate([x[:, :LANES] for x in w], axis=1).reshape(ng, SUBLANES, nm)
    xi = jnp.concatenate([x[:, LANES:] for x in w], axis=1).reshape(ng, SUBLANES, nm)
    for s, kstep in enumerate((1, 2, 4)):
        ar, ai = tab_ref[2 * s], tab_ref[2 * s + 1]
        sr = pltpu.roll(xr, kstep, axis=1)
        sim = pltpu.roll(xi, kstep, axis=1)
        xr, xi = xr + ar * sr - ai * sim, xi + ar * sim + ai * sr
    pr, pi = tab_ref[6], tab_ref[7]
    first_row = lax.broadcasted_iota(jnp.int32, (SUBLANES, nm), 0) == 0
    if not streams:
        cr, ci = car_sc[:, :nm], car_sc[:, nm:]
    prev_r, prev_i = [], []
    for g in range(ng):
        if streams:
            cr = jnp.broadcast_to(s50_ref[0, g:g + 1, :nm], (SUBLANES, nm))
            ci = jnp.broadcast_to(s50_ref[0, g:g + 1, nm:], (SUBLANES, nm))
        yr = xr[g] + pr * cr - pi * ci
        yi = xi[g] + pr * ci + pi * cr
        prev_r.append(jnp.where(first_row, cr, pltpu.roll(yr, 1, axis=0)))
        prev_i.append(jnp.where(first_row, ci, pltpu.roll(yi, 1, axis=0)))
        if streams:
            s5o_ref[0, g:g + 1, :nm] = yr[SUBLANES - 1:SUBLANES]
            s5o_ref[0, g:g + 1, nm:] = yi[SUBLANES - 1:SUBLANES]
        else:
            cr = jnp.broadcast_to(yr[SUBLANES - 1:SUBLANES], (SUBLANES, nm))
            ci = jnp.broadcast_to(yi[SUBLANES - 1:SUBLANES], (SUBLANES, nm))
    if not streams:
        car_sc[:, :nm] = cr
        car_sc[:, nm:] = ci
        s5o_ref[0] = car_sc[0:1, :]
    prev_r = jnp.concatenate(prev_r, axis=0).astype(BF16)
    prev_i = jnp.concatenate(prev_i, axis=0).astype(BF16)
    ys = []
    for q in range(S5_PAIRS):
        ms = slice(PAIR_MODES * q, PAIR_MODES * (q + 1))
        lhs = jnp.concatenate([prev_r[:, ms], prev_i[:, ms], uq[q]], axis=1)
        ys.append(_dot(lhs, jnp.concatenate([w3xr_ref[q], w3xi_ref[q], w3u_ref[q]], axis=0)))
    conv = _s5_regroup_out(ys)
    for j in range(S5_STEP):
        for c in range(S5_SLABS):
            y = conv[j][c] + d_ref[:, LANES * c:LANES * (c + 1)] * step_slab(j, c)
            inner = math.sqrt(2.0 / math.pi) * (y + 0.044715 * (y * y * y))
            g5_sc[c, pl.ds(j, tg, stride=S5_STEP), :] = 0.5 * y * (1.0 + jnp.tanh(inner))
    g5_ref[...] = jnp.concatenate([g5_sc[c] for c in range(S5_SLABS)], axis=1).astype(BF16)


def _mix(q, k, la, v, r, u5, st0, s50, p, nb, tb, unroll, streams):
    n = q.shape[0]
    nt = n // (nb * tb)
    nst = st0.shape[1]
    nseq = s50.shape[1]
    row = lambda w: pl.BlockSpec((tb, w), lambda b, t: (b * nt + t, 0))
    slab = pl.BlockSpec((S5_SLABS, tb, LANES), lambda b, t: (0, b * nt + t, 0))
    st_spec = pl.BlockSpec((1, nst, GLA_DV, LANES), lambda b, t: (b, 0, 0, 0))
    s5_spec = pl.BlockSpec((1, nseq, 2 * S5_MODES), lambda b, t: (b, 0, 0))
    pair_tile = _const_spec((S5_PAIRS, PAIR_MODES, MXU_TILE))
    in_specs = [row(GLA_KEY), row(GLA_KEY), row(GLA_KEY), row(GLA_VAL), row(GLA_VAL), slab,
                st_spec, s5_spec, _const_spec((1, GLA_VAL)),
                pair_tile, pair_tile, pair_tile, pair_tile, _const_spec((S5_PAIRS, MXU_TILE, MXU_TILE)),
                _const_spec((1, S5_WIDTH)), _const_spec((N_SCAN_TABLES, SUBLANES, S5_MODES))]
    out_shape = [jax.ShapeDtypeStruct((n, GLA_VAL), BF16), jax.ShapeDtypeStruct((n, S5_WIDTH), BF16),
                 jax.ShapeDtypeStruct(st0.shape, F32), jax.ShapeDtypeStruct(s50.shape, F32)]
    return pl.pallas_call(
        functools.partial(_mix_kernel, tb=tb, unroll=unroll, streams=streams),
        out_shape=out_shape,
        grid=(nb, nt),
        in_specs=in_specs,
        out_specs=[row(GLA_VAL), row(S5_WIDTH), st_spec, s5_spec],
        scratch_shapes=[pltpu.VMEM((GLA_HEADS, GLA_DV, LANES), F32),
                        pltpu.VMEM((SUBLANES, 2 * S5_MODES), F32),
                        pltpu.VMEM((tb, GLA_VAL), F32),
                        pltpu.VMEM((S5_SLABS, tb, LANES), F32)],
        compiler_params=pltpu.CompilerParams(dimension_semantics=("arbitrary", "arbitrary"),
                                             vmem_limit_bytes=VMEM_LIMIT),
        name="mix",
    )(q, k, la, v, r, u5, st0, s50, p['g_gla_head'], *p['s5_tiles'], p['s5_d'], p['tab'])


def _back_kernel(h_ref, og_ref, g5_ref, gmg_ref, gms_ref, wgo_ref, wa_ref, wb_ref, wso_ref, wo_ref,
                 g2_ref, wg_ref, wu_ref, wd_ref, gf_ref, y_ref):
    y_gla = _dot(og_ref[...], wgo_ref[...])
    g5 = g5_ref[...]
    glu = _dot(g5, wa_ref[...]) * _sigmoid(_dot(g5, wb_ref[...]))
    y_s5 = _dot(glu.astype(BF16), wso_ref[...])
    m = _sigmoid(gmg_ref[...]) * y_gla + _sigmoid(gms_ref[...]) * y_s5
    h = h_ref[...] + _dot(m.astype(BF16), wo_ref[...])
    hn = _rms(h, g2_ref[...]).astype(BF16)
    h = h + 0.5 * _swiglu(hn, wg_ref, wu_ref, wd_ref)
    y_ref[...] = _rms(h, gf_ref[...])


def _back(h, og, g5, gmg, gms, p, tm):
    n = h.shape[0]
    row = lambda w: pl.BlockSpec((tm, w), lambda i: (i, 0))
    in_specs = [row(D_MODEL), row(GLA_VAL), row(S5_WIDTH), row(D_MODEL), row(D_MODEL),
                _const_spec((GLA_VAL, D_MODEL)), _const_spec((S5_WIDTH, S5_WIDTH)), _const_spec((S5_WIDTH, S5_WIDTH)),
                _const_spec((S5_WIDTH, D_MODEL)), _const_spec((D_MODEL, D_MODEL)), _const_spec((1, D_MODEL)),
                _const_spec((D_MODEL, D_FF)), _const_spec((D_MODEL, D_FF)), _const_spec((D_FF, D_MODEL)),
                _const_spec((1, D_MODEL))]
    return pl.pallas_call(
        _back_kernel,
        out_shape=jax.ShapeDtypeStruct((n, D_MODEL), F32),
        grid=(n // tm,),
        in_specs=in_specs,
        out_specs=row(D_MODEL),
        compiler_params=pltpu.CompilerParams(dimension_semantics=("parallel",), vmem_limit_bytes=VMEM_LIMIT),
        name="back",
    )(h, og, g5, gmg, gms, p['w_gla_out'], p['w_glu_a'], p['w_glu_b'], p['w_s5_out'], p['w_out'],
      p['g_ffn2'], p['wg2'], p['wu2'], p['wd2'], p['g_final'])


def _gla_state_in(s, nb):
    st = jnp.swapaxes(s.astype(F32), 2, 3)
    z = jnp.zeros_like(st)
    sel = (jnp.arange(GLA_HEADS) % 2 == 0)[None, :, None, None]
    st = jnp.where(sel, jnp.concatenate([st, z], axis=-1), jnp.concatenate([z, st], axis=-1))
    return st.reshape(nb, -1, GLA_DV, LANES)


def _gla_state_out(st, bsz):
    st = st.reshape(bsz, GLA_HEADS, GLA_DV, LANES)
    sel = (jnp.arange(GLA_HEADS) % 2 == 0)[None, :, None, None]
    return jnp.swapaxes(jnp.where(sel, st[..., :GLA_DK], st[..., GLA_DK:]), 2, 3)


def _s5_state_in(x0, nb):
    x = x0.astype(F32).reshape(x0.shape[0], S5_MODES, 2)
    return jnp.concatenate([x[..., 0], x[..., 1]], axis=-1).reshape(nb, -1, 2 * S5_MODES)


def _s5_state_out(x, bsz):
    x = x.reshape(bsz, 2, S5_MODES)
    return jnp.stack([x[:, 0], x[:, 1]], axis=-1).reshape(bsz, S5_GROUPS, S5_STATE, 2)


def _layer(x, s_gla, s_s5, p, tm, tb, unroll, streams):
    bsz, t, _ = x.shape
    n = bsz * t
    nb = n // tb if streams else bsz
    h, q, k, v, r, u5, gmg, gms, la = _front(x.reshape(n, D_MODEL), p, tm)
    og, g5, st, s5 = _mix(q, k, la, v, r, u5, _gla_state_in(s_gla, nb), _s5_state_in(s_s5, nb), p,
                          nb, tb, unroll, streams)
    y = _back(h, og, g5, gmg, gms, p, tm)
    return y.reshape(bsz, t, D_MODEL), _gla_state_out(st, bsz), _s5_state_out(s5, bsz)


def kernel(x_prompt, x_sample, state_gla, state_s5, g_ffn1, w_ffn1_gate, w_ffn1_up, w_ffn1_down, g_mix, w_in, w_gate_up, b_gate, g_gla_head, w_gla_out, s5_lam_re, s5_lam_im, s5_log_dt, s5_b_re, s5_b_im, s5_c_re, s5_c_im, s5_d, w_glu_a, w_glu_b, w_s5_out, w_out, g_ffn2, w_ffn2_gate, w_ffn2_up, w_ffn2_down, g_final):
    vec = lambda a: a.astype(F32).reshape(1, -1)
    sizes = (GLA_KEY, GLA_KEY, GLA_VAL, GLA_VAL, GLA_GATE_RANK, S5_WIDTH, D_MODEL, D_MODEL)
    offs = [sum(sizes[:i]) for i in range(len(sizes) + 1)]
    wq, wk, wv, wr, wga, wu5, wgg, wgs = [w_in[:, offs[i]:offs[i + 1]] for i in range(len(sizes))]
    *s5_tiles, tab = _s5_prep(s5_lam_re, s5_lam_im, s5_log_dt, s5_b_re, s5_b_im, s5_c_re, s5_c_im)
    p = {
        'g_ffn1': vec(g_ffn1), 'wg1': w_ffn1_gate.astype(BF16), 'wu1': w_ffn1_up.astype(BF16),
        'wd1': w_ffn1_down.astype(BF16), 'g_mix': vec(g_mix),
        'w_main': jnp.concatenate([wq, wk, wv, wr, wu5, wgg, wgs], axis=1).astype(BF16),
        'w_ga': jnp.pad(wga, ((0, 0), (0, LANES - GLA_GATE_RANK))).astype(BF16),
        'w_gu': jnp.pad(w_gate_up, ((0, LANES - GLA_GATE_RANK), (0, 0))).astype(BF16),
        'b_gate': vec(b_gate), 'g_gla_head': vec(g_gla_head),
        's5_tiles': s5_tiles, 'tab': tab, 's5_d': vec(s5_d),
        'w_gla_out': w_gla_out.astype(BF16), 'w_glu_a': w_glu_a.astype(BF16), 'w_glu_b': w_glu_b.astype(BF16),
        'w_s5_out': w_s5_out.astype(BF16), 'w_out': w_out.astype(BF16),
        'g_ffn2': vec(g_ffn2), 'wg2': w_ffn2_gate.astype(BF16), 'wu2': w_ffn2_up.astype(BF16),
        'wd2': w_ffn2_down.astype(BF16), 'g_final': vec(g_final),
    }
    bp = x_prompt.shape[0]
    zero_gla = jnp.zeros((bp, GLA_HEADS, GLA_DK, GLA_DV), F32)
    zero_s5 = jnp.zeros((bp, S5_GROUPS, S5_STATE, 2), F32)
    y_p, gla_p, s5_p = _layer(x_prompt, zero_gla, zero_s5, p, tm=256, tb=1024, unroll=4, streams=False)
    n_sample = x_sample.shape[0] * x_sample.shape[1]
    y_s, gla_s, s5_s = _layer(x_sample, state_gla, state_s5, p, tm=256, tb=n_sample, unroll=4, streams=True)
    return (y_p, y_s, gla_p, s5_p, gla_s, s5_s)
```

```python
import functools
import math

import jax
import jax.numpy as jnp
from jax import lax
from jax.experimental import pallas as pl
from jax.experimental.pallas import tpu as pltpu

F32 = jnp.float32
BF16 = jnp.bfloat16

D_MODEL = 1024
D_FF = 2816
CHUNK = 64
SUB = 16
GLA_HEADS = 4
GLA_DK = 64
GLA_DV = 128
GLA_KEY = GLA_HEADS * GLA_DK
GLA_VAL = GLA_HEADS * GLA_DV
GLA_GATE_RANK = 16
GLA_GATE_TAU = 16.0
S5_WIDTH = 512
S5_GROUP = 16
S5_GROUPS = 32
S5_STATE = 64
S5_MODES = S5_GROUPS * S5_STATE
EPS = 1e-6
SUBLANES = 8
LANES = 128
MXU_TILE = 256
S5_STEP = SUBLANES
S5_PAIRS = S5_GROUPS // 2
PAIR_CH = 2 * S5_GROUP
PAIR_MODES = 2 * S5_STATE
S5_SLABS = S5_WIDTH // LANES
N_SCAN_TABLES = 8
EXP_CLAMP = 80.0
FF_CHUNK = 512
VMEM_LIMIT = 60 * 1024 * 1024

_OQ, _OK, _OV, _OR, _OU, _OGG, _OGS = 0, 256, 512, 1024, 1536, 2048, 3072
W_MAIN_COLS = 4096


def _dot(a, b):
    return jnp.dot(a, b, preferred_element_type=F32)


def _dot_nt(a, b):
    return lax.dot_general(a, b, (((1,), (1,)), ((), ())), preferred_element_type=F32)


def _split_dot(a, b):
    a1 = a.astype(BF16)
    a2 = (a - a1.astype(F32)).astype(BF16)
    b1 = b.astype(BF16)
    b2 = (b - b1.astype(F32)).astype(BF16)
    return _dot(a1, b1) + _dot(a1, b2) + _dot(a2, b1)


def _cmul(ar, ai, br, bi):
    return ar * br - ai * bi, ar * bi + ai * br


def _sigmoid(x):
    return 1.0 / (1.0 + jnp.exp(-x))


def _rms(x, g):
    ms = jnp.mean(x * x, axis=-1, keepdims=True)
    return x * lax.rsqrt(ms + EPS) * g


def _ff_chunks():
    out, c0 = [], 0
    while c0 < D_FF:
        n = min(FF_CHUNK, D_FF - c0)
        out.append((c0, n))
        c0 += n
    return out


def _swiglu(xn, wg_ref, wu_ref, wd_ref):
    acc = None
    for c0, n in _ff_chunks():
        g = _dot(xn, wg_ref[:, c0:c0 + n])
        u = _dot(xn, wu_ref[:, c0:c0 + n])
        a = (g * _sigmoid(g) * u).astype(BF16)
        part = _dot(a, wd_ref[c0:c0 + n, :])
        acc = part if acc is None else acc + part
    return acc


def _const_spec(shape):
    nd = len(shape)
    return pl.BlockSpec(shape, lambda *_: (0,) * nd, pipeline_mode=pl.Buffered(1))


def _front_kernel(x_ref, g1_ref, wg_ref, wu_ref, wd_ref, gmix_ref, win_ref, wga_ref, wgu_ref, bg_ref,
                  h_ref, q_ref, k_ref, v_ref, r_ref, u5_ref, gmg_ref, gms_ref, la_ref, *, sub):
    for r0 in range(0, x_ref.shape[0], sub):
        rows = slice(r0, r0 + sub)
        x = x_ref[rows, :]
        xn = _rms(x, g1_ref[...]).astype(BF16)
        h = x + 0.5 * _swiglu(xn, wg_ref, wu_ref, wd_ref)
        h_ref[rows, :] = h
        un = _rms(h, gmix_ref[...]).astype(BF16)

        def proj(c0, n):
            return _dot(un, win_ref[:, c0:c0 + n])

        q_ref[rows, :] = proj(_OQ, GLA_KEY) * (GLA_DK ** -0.5)
        k_ref[rows, :] = proj(_OK, GLA_KEY)
        v_ref[rows, :] = proj(_OV, GLA_VAL).astype(BF16)
        r_ref[rows, :] = proj(_OR, GLA_VAL)
        u5 = proj(_OU, S5_WIDTH)
        for c in range(S5_SLABS):
            u5_ref[c, rows, :] = u5[:, LANES * c:LANES * (c + 1)]
        gmg_ref[rows, :] = proj(_OGG, D_MODEL)
        gms_ref[rows, :] = proj(_OGS, D_MODEL)
        ga = _dot(un, wga_ref[...])
        pre = _dot(ga.astype(BF16), wgu_ref[...]) + bg_ref[...]
        log_sig = jnp.minimum(pre, 0.0) - jnp.log(1.0 + jnp.exp(-jnp.abs(pre)))
        la_ref[rows, :] = log_sig * (1.0 / GLA_GATE_TAU)


def _front(x2d, p, tm, sub):
    n = x2d.shape[0]
    row = lambda w: pl.BlockSpec((tm, w), lambda i: (i, 0))
    slab = pl.BlockSpec((S5_SLABS, tm, LANES), lambda i: (0, i, 0))
    in_specs = [row(D_MODEL), _const_spec((1, D_MODEL)),
                _const_spec((D_MODEL, D_FF)), _const_spec((D_MODEL, D_FF)), _const_spec((D_FF, D_MODEL)),
                _const_spec((1, D_MODEL)), _const_spec((D_MODEL, W_MAIN_COLS)),
                _const_spec((D_MODEL, LANES)), _const_spec((LANES, GLA_KEY)), _const_spec((1, GLA_KEY))]
    sds = lambda w, dt: jax.ShapeDtypeStruct((n, w), dt)
    out_shape = [sds(D_MODEL, F32), sds(GLA_KEY, F32), sds(GLA_KEY, F32), sds(GLA_VAL, BF16), sds(GLA_VAL, F32),
                 jax.ShapeDtypeStruct((S5_SLABS, n, LANES), F32), sds(D_MODEL, F32), sds(D_MODEL, F32),
                 sds(GLA_KEY, F32)]
    out_specs = [row(D_MODEL), row(GLA_KEY), row(GLA_KEY), row(GLA_VAL), row(GLA_VAL), slab,
                 row(D_MODEL), row(D_MODEL), row(GLA_KEY)]
    return pl.pallas_call(
        functools.partial(_front_kernel, sub=sub),
        out_shape=out_shape,
        grid=(n // tm,),
        in_specs=in_specs,
        out_specs=out_specs,
        compiler_params=pltpu.CompilerParams(dimension_semantics=("parallel",), vmem_limit_bytes=VMEM_LIMIT),
        name="front",
    )(x2d, p['g_ffn1'], p['wg1'], p['wu1'], p['wd1'], p['g_mix'], p['w_main'], p['w_ga'], p['w_gu'], p['b_gate'])


def _shift_lanes_256(x, s):
    lane = lax.broadcasted_iota(jnp.int32, (1, LANES), 1)
    lo, hi = x[:, :LANES], x[:, LANES:]
    if s == 0:
        return x
    if s >= LANES:
        r = s - LANES
        moved = lo if r == 0 else jnp.where(lane >= r, pltpu.roll(lo, r, axis=1), 0.0)
        return jnp.concatenate([jnp.zeros_like(lo), moved], axis=1)
    rl, rh = pltpu.roll(lo, s, axis=1), pltpu.roll(hi, s, axis=1)
    return jnp.concatenate([jnp.where(lane >= s, rl, 0.0), jnp.where(lane >= s, rh, rl)], axis=1)


def _s5_prep_kernel(lrc_ref, lic_ref, ldtc_ref, lrr_ref, lir_ref, ldtr_ref, br_ref, bi_ref, cr_ref, ci_ref,
                    w1r_ref, w1i_ref, w3xr_ref, w3xi_ref, w3u_ref, tab_ref):
    lr, li = lrc_ref[...], lic_ref[...]
    dt = jnp.exp(ldtc_ref[...])
    ar, th = lr * dt, li * dt

    def powers(kvec):
        mag = jnp.exp(ar * kvec)
        ang = th * kvec
        return mag * jnp.cos(ang), mag * jnp.sin(ang)

    lane = lax.broadcasted_iota(jnp.int32, (1, MXU_TILE), 1)
    step = lax.shift_right_logical(lane, 5)
    lane_group = lax.shift_right_logical(lane, 4) & 1
    mode = lax.broadcasted_iota(jnp.int32, (S5_MODES, 1), 0)
    own = lane_group == (lax.shift_right_logical(mode, 6) & 1)
    chan = lax.broadcasted_iota(jnp.int32, (S5_GROUP, 1), 0)
    rep = jnp.where((lane & (S5_GROUP - 1)) == chan, 1.0, 0.0)
    b_r, b_i = _split_dot(br_ref[...], rep), _split_dot(bi_ref[...], rep)
    c_r, c_i = _split_dot(cr_ref[...], rep), _split_dot(ci_ref[...], rep)

    ab_re, ab_im = powers(jnp.ones((1, 1), F32))
    nr, ni = ab_re - 1.0, ab_im
    den = lr * lr + li * li
    f_re = (nr * lr + ni * li) / den
    f_im = (ni * lr - nr * li) / den
    bbr, bbi = _cmul(f_re, f_im, b_r, b_i)

    p7r, p7i = powers((S5_STEP - 1 - step).astype(F32))
    w1r, w1i = _cmul(p7r, p7i, bbr, bbi)
    w1r_ref[...] = jnp.where(own, w1r, 0.0).astype(BF16)
    w1i_ref[...] = jnp.where(own, w1i, 0.0).astype(BF16)
    p1r, p1i = powers((step + 1).astype(F32))
    xr, xi = _cmul(p1r, p1i, c_r, c_i)
    w3xr_ref[...] = jnp.where(own, xr, 0.0).astype(BF16)
    w3xi_ref[...] = jnp.where(own, -xi, 0.0).astype(BF16)

    p0r, p0i = powers(step.astype(F32))
    car, cai = _cmul(p0r, p0i, c_r, c_i)
    car, cai = jnp.where(own, car, 0.0), jnp.where(own, cai, 0.0)
    lane_w = lax.broadcasted_iota(jnp.int32, (1, S5_WIDTH), 1)
    rep_w = jnp.where((lane_w & (S5_GROUP - 1)) == chan, 1.0, 0.0)
    own_w = lax.shift_right_logical(lane_w, 4) == lax.shift_right_logical(mode, 6)
    dbr, dbi = _cmul(f_re, f_im, _split_dot(br_ref[...], rep_w), _split_dot(bi_ref[...], rep_w))
    dbr, dbi = jnp.where(own_w, dbr, 0.0), jnp.where(own_w, dbi, 0.0)
    taps = _split_dot(dbr.T, car) - _split_dot(dbi.T, cai)
    for i in range(S5_STEP):
        blk = _shift_lanes_256(taps, PAIR_CH * i)
        w3u_ref[:, PAIR_CH * i:PAIR_CH * (i + 1), :] = blk.reshape(S5_PAIRS, PAIR_CH, MXU_TILE).astype(BF16)

    arr = lrr_ref[...] * jnp.exp(ldtr_ref[...])
    thr = lir_ref[...] * jnp.exp(ldtr_ref[...])
    pos = lax.broadcasted_iota(jnp.int32, (SUBLANES, 1), 0)

    def powers_rows(kcol):
        mag = jnp.exp(arr * kcol)
        ang = thr * kcol
        return mag * jnp.cos(ang), mag * jnp.sin(ang)

    for s, kstep in enumerate((1, 2, 4)):
        kr, ki = powers_rows(jnp.full((1, 1), float(S5_STEP * kstep), F32))
        tab_ref[2 * s] = jnp.where(pos >= kstep, kr, 0.0)
        tab_ref[2 * s + 1] = jnp.where(pos >= kstep, ki, 0.0)
    tab_ref[6], tab_ref[7] = powers_rows(((pos + 1) * S5_STEP).astype(F32))


def _s5_prep(lam_re, lam_im, log_dt, b_re, b_im, c_re, c_im):
    m = S5_MODES
    ldt = jnp.broadcast_to(log_dt.astype(F32)[:, None], (S5_GROUPS, S5_STATE))
    col = lambda a: a.astype(F32).reshape(m, 1)
    rowv = lambda a: a.astype(F32).reshape(1, m)
    bmat = lambda a: a.astype(F32).reshape(m, S5_GROUP)
    cmat = lambda a: a.astype(F32).transpose(0, 2, 1).reshape(m, S5_GROUP)
    tile = jax.ShapeDtypeStruct((m, MXU_TILE), BF16)
    out_shape = [tile] * 4 + [jax.ShapeDtypeStruct((S5_PAIRS, MXU_TILE, MXU_TILE), BF16),
                              jax.ShapeDtypeStruct((N_SCAN_TABLES, SUBLANES, m), F32)]
    w1r, w1i, w3xr, w3xi, w3u, tab = pl.pallas_call(
        _s5_prep_kernel,
        out_shape=out_shape,
        compiler_params=pltpu.CompilerParams(vmem_limit_bytes=VMEM_LIMIT),
        name="s5_prep",
    )(col(lam_re), col(lam_im), col(ldt), rowv(lam_re), rowv(lam_im), rowv(ldt),
      bmat(b_re), bmat(b_im), cmat(c_re), cmat(c_im))
    per_pair = lambda a: a.reshape(S5_PAIRS, PAIR_MODES, MXU_TILE)
    return per_pair(w1r), per_pair(w1i), per_pair(w3xr), per_pair(w3xi), w3u, tab


def _lane_window(lo):
    lane = lax.broadcasted_iota(jnp.int32, (1, LANES), 1)
    return (lane >= lo) & (lane < lo + PAIR_CH)


def _merge_windows(pieces):
    acc = pieces[0]
    for k in range(1, len(pieces)):
        acc = jnp.where(_lane_window(PAIR_CH * k), pieces[k], acc)
    return acc


def _rolled(cache, key, make, shift):
    if (key, shift) not in cache:
        x = make()
        cache[(key, shift)] = x if shift == 0 else pltpu.roll(x, shift, axis=1)
    return cache[(key, shift)]


def _s5_regroup_in(step_slab):
    per_tile = LANES // PAIR_CH
    cache, outs = {}, []
    for q in range(S5_PAIRS):
        halves = []
        for m in range(MXU_TILE // LANES):
            pieces = []
            for i in range(per_tile * m, per_tile * (m + 1)):
                c = q // per_tile
                pieces.append(_rolled(cache, (i, c), lambda i=i, c=c: step_slab(i, c),
                                      PAIR_CH * ((i - q) % per_tile)))
            halves.append(_merge_windows(pieces))
        outs.append(jnp.concatenate(halves, axis=1))
    return outs


def _s5_regroup_out(ys):
    per_tile = LANES // PAIR_CH
    cache, out = {}, []
    for j in range(S5_STEP):
        slabs = []
        for c in range(S5_SLABS):
            pieces = []
            for q in range(per_tile * c, per_tile * (c + 1)):
                m = j // per_tile
                pieces.append(_rolled(cache, (q, m), lambda q=q, m=m: ys[q][:, LANES * m:LANES * (m + 1)],
                                      PAIR_CH * ((q - j) % per_tile)))
            slabs.append(_merge_windows(pieces))
        out.append(slabs)
    return out


def _mix_kernel(q_ref, k_ref, la_ref, v_ref, r_ref, u5_ref, st0_ref, s50_ref, gh_ref,
                w1r_ref, w1i_ref, w3xr_ref, w3xi_ref, w3u_ref, d_ref, tab_ref,
                og_ref, g5_ref, sto_ref, s5o_ref,
                st_sc, car_sc, o_sc, g5_sc, *, tb, unroll, streams):
    nm = S5_MODES

    if not streams:
        @pl.when(pl.program_id(1) == 0)
        def _():
            st_sc[...] = st0_ref[0]
            car_sc[...] = jnp.broadcast_to(s50_ref[0], (SUBLANES, 2 * nm))

    ti = lax.broadcasted_iota(jnp.int32, (CHUNK, CHUNK), 0)
    si = lax.broadcasted_iota(jnp.int32, (CHUNK, CHUNK), 1)
    causal = si <= ti
    tril = jnp.where(causal, 1.0, 0.0).astype(BF16)
    lane = lax.broadcasted_iota(jnp.int32, (1, GLA_KEY), 1) % LANES
    head_mask = (lane < GLA_DK, lane >= GLA_DK)
    nsub = CHUNK // SUB

    def chunk(c):
        rows = pl.ds(c * CHUNK if isinstance(c, int) else pl.multiple_of(c * CHUNK, CHUNK), CHUNK)
        la = la_ref[rows, :]
        hi = la.astype(BF16)
        lo = (la - hi.astype(F32)).astype(BF16)
        b = _dot(tril, hi) + _dot(tril, lo)
        refs = [jnp.zeros((1, GLA_KEY), F32)] + [b[SUB * i - 1:SUB * i, :] for i in range(1, nsub)]
        rb = jnp.concatenate([jnp.broadcast_to(x, (SUB, GLA_KEY)) for x in refs], axis=0)
        q, k = q_ref[rows, :], k_ref[rows, :]
        b_last = b[CHUNK - 1:CHUNK, :]
        qt = q * jnp.exp(b - rb)
        qd = q * jnp.exp(b)
        qt = [jnp.where(m, qt, 0.0).astype(BF16) for m in head_mask]
        qd = [jnp.where(m, qd, 0.0).astype(BF16) for m in head_mask]
        kd = (k * jnp.exp(b_last - b)).astype(BF16)
        e_last = jnp.exp(b_last)
        kts = [(k * jnp.exp(jnp.minimum(x - b, EXP_CLAMP))).astype(BF16) for x in refs]
        for h in range(GLA_HEADS):
            pc = slice(LANES * (h // 2), LANES * (h // 2 + 1))
            qt_h = qt[h % 2][:, pc]
            sc = jnp.concatenate(
                [_dot_nt(qt_h[SUB * i:SUB * (i + 1)], kts[i][:, pc]) for i in range(nsub)], axis=0)
            pm = jnp.where(causal, sc, 0.0).astype(BF16)
            vh = v_ref[rows, GLA_DV * h:GLA_DV * (h + 1)]
            st = st0_ref[0, c * GLA_HEADS + h] if streams else st_sc[h]
            o = _dot(pm, vh) + _dot_nt(qd[h % 2][:, pc], st.astype(BF16))
            o_sc[rows, GLA_DV * h:GLA_DV * (h + 1)] = o
            vt = vh.astype(F32).T.astype(BF16)
            st_new = st * e_last[:, pc] + _dot(vt, kd[:, pc])
            if streams:
                sto_ref[0, c * GLA_HEADS + h] = st_new
            else:
                st_sc[h] = st_new

    nchunks = tb // CHUNK
    if nchunks <= unroll:
        for c in range(nchunks):
            chunk(c)
    else:
        def body(i, carry):
            for j in range(unroll):
                chunk(i * unroll + j)
            return carry
        lax.fori_loop(0, nchunks // unroll, body, 0)

    o_all = o_sc[...]
    normed = []
    for h in range(GLA_HEADS):
        oh = o_all[:, GLA_DV * h:GLA_DV * (h + 1)]
        normed.append(oh * lax.rsqrt(jnp.mean(oh * oh, axis=-1, keepdims=True) + EPS))
    r = r_ref[...]
    og = jnp.concatenate(normed, axis=1) * gh_ref[...] * (r * _sigmoid(r))
    og_ref[...] = og.astype(BF16)
    if not streams:
        sto_ref[0] = st_sc[...]

    tg = tb // S5_STEP
    ng = tg // SUBLANES
    step_slab = lambda i, c: u5_ref[c, pl.ds(i, tg, stride=S5_STEP), :]
    uq = [x.astype(BF16) for x in _s5_regroup_in(step_slab)]
    w = [_dot_nt(uq[q], jnp.concatenate([w1r_ref[q], w1i_ref[q]], axis=0)) for q in range(S5_PAIRS)]
    xr = jnp.concatenate([x[:, :LANES] for x in w], axis=1).reshape(ng, SUBLANES, nm)
    xi = jnp.concatenate([x[:, LANES:] for x in w], axis=1).reshape(ng, SUBLANES, nm)
    for s, kstep in enumerate((1, 2, 4)):
        ar, ai = tab_ref[2 * s], tab_ref[2 * s + 1]
        sr = pltpu.roll(xr, kstep, axis=1)
        sim = pltpu.roll(xi, kstep, axis=1)
        xr, xi = xr + ar * sr - ai * sim, xi + ar * sim + ai * sr
    pr, pi = tab_ref[6], tab_ref[7]
    first_row = lax.broadcasted_iota(jnp.int32, (SUBLANES, nm), 0) == 0
    if not streams:
        cr, ci = car_sc[:, :nm], car_sc[:, nm:]
    prev_r, prev_i = [], []
    for g in range(ng):
        if streams:
            cr = jnp.broadcast_to(s50_ref[0, g:g + 1, :nm], (SUBLANES, nm))
            ci = jnp.broadcast_to(s50_ref[0, g:g + 1, nm:], (SUBLANES, nm))
        yr = xr[g] + pr * cr - pi * ci
        yi = xi[g] + pr * ci + pi * cr
        prev_r.append(jnp.where(first_row, cr, pltpu.roll(yr, 1, axis=0)))
        prev_i.append(jnp.where(first_row, ci, pltpu.roll(yi, 1, axis=0)))
        if streams:
            s5o_ref[0, g:g + 1, :nm] = yr[SUBLANES - 1:SUBLANES]
            s5o_ref[0, g:g + 1, nm:] = yi[SUBLANES - 1:SUBLANES]
        else:
            cr = jnp.broadcast_to(yr[SUBLANES - 1:SUBLANES], (SUBLANES, nm))
            ci = jnp.broadcast_to(yi[SUBLANES - 1:SUBLANES], (SUBLANES, nm))
    if not streams:
        car_sc[:, :nm] = cr
        car_sc[:, nm:] = ci
        s5o_ref[0] = car_sc[0:1, :]
    prev_r = jnp.concatenate(prev_r, axis=0).astype(BF16)
    prev_i = jnp.concatenate(prev_i, axis=0).astype(BF16)
    ys = []
    for q in range(S5_PAIRS):
        ms = slice(PAIR_MODES * q, PAIR_MODES * (q + 1))
        lhs = jnp.concatenate([prev_r[:, ms], prev_i[:, ms], uq[q]], axis=1)
        ys.append(_dot(lhs, jnp.concatenate([w3xr_ref[q], w3xi_ref[q], w3u_ref[q]], axis=0)))
    conv = _s5_regroup_out(ys)
    for j in range(S5_STEP):
        for c in range(S5_SLABS):
            y = conv[j][c] + d_ref[:, LANES * c:LANES * (c + 1)] * step_slab(j, c)
            inner = math.sqrt(2.0 / math.pi) * (y + 0.044715 * (y * y * y))
            g5_sc[c, pl.ds(j, tg, stride=S5_STEP), :] = 0.5 * y * (1.0 + jnp.tanh(inner))
    g5_ref[...] = jnp.concatenate([g5_sc[c] for c in range(S5_SLABS)], axis=1).astype(BF16)


def _mix(q, k, la, v, r, u5, st0, s50, p, nb, tb, unroll, streams):
    n = q.shape[0]
    nt = n // (nb * tb)
    nst = st0.shape[1]
    nseq = s50.shape[1]
    row = lambda w: pl.BlockSpec((tb, w), lambda b, t: (b * nt + t, 0))
    slab = pl.BlockSpec((S5_SLABS, tb, LANES), lambda b, t: (0, b * nt + t, 0))
    st_spec = pl.BlockSpec((1, nst, GLA_DV, LANES), lambda b, t: (b, 0, 0, 0))
    s5_spec = pl.BlockSpec((1, nseq, 2 * S5_MODES), lambda b, t: (b, 0, 0))
    pair_tile = _const_spec((S5_PAIRS, PAIR_MODES, MXU_TILE))
    in_specs = [row(GLA_KEY), row(GLA_KEY), row(GLA_KEY), row(GLA_VAL), row(GLA_VAL), slab,
                st_spec, s5_spec, _const_spec((1, GLA_VAL)),
                pair_tile, pair_tile, pair_tile, pair_tile, _const_spec((S5_PAIRS, MXU_TILE, MXU_TILE)),
                _const_spec((1, S5_WIDTH)), _const_spec((N_SCAN_TABLES, SUBLANES, S5_MODES))]
    out_shape = [jax.ShapeDtypeStruct((n, GLA_VAL), BF16), jax.ShapeDtypeStruct((n, S5_WIDTH), BF16),
                 jax.ShapeDtypeStruct(st0.shape, F32), jax.ShapeDtypeStruct(s50.shape, F32)]
    return pl.pallas_call(
        functools.partial(_mix_kernel, tb=tb, unroll=unroll, streams=streams),
        out_shape=out_shape,
        grid=(nb, nt),
        in_specs=in_specs,
        out_specs=[row(GLA_VAL), row(S5_WIDTH), st_spec, s5_spec],
        scratch_shapes=[pltpu.VMEM((GLA_HEADS, GLA_DV, LANES), F32),
                        pltpu.VMEM((SUBLANES, 2 * S5_MODES), F32),
                        pltpu.VMEM((tb, GLA_VAL), F32),
                        pltpu.VMEM((S5_SLABS, tb, LANES), F32)],
        compiler_params=pltpu.CompilerParams(dimension_semantics=("arbitrary", "arbitrary"),
                                             vmem_limit_bytes=VMEM_LIMIT),
        name="mix",
    )(q, k, la, v, r, u5, st0, s50, p['g_gla_head'], *p['s5_tiles'], p['s5_d'], p['tab'])


def _back_kernel(h_ref, og_ref, g5_ref, gmg_ref, gms_ref, wgo_ref, wa_ref, wb_ref, wso_ref, wo_ref,
                 g2_ref, wg_ref, wu_ref, wd_ref, gf_ref, y_ref, *, sub):
    for r0 in range(0, h_ref.shape[0], sub):
        rows = slice(r0, r0 + sub)
        y_gla = _dot(og_ref[rows, :], wgo_ref[...])
        g5 = g5_ref[rows, :]
        glu = _dot(g5, wa_ref[...]) * _sigmoid(_dot(g5, wb_ref[...]))
        y_s5 = _dot(glu.astype(BF16), wso_ref[...])
        m = _sigmoid(gmg_ref[rows, :]) * y_gla + _sigmoid(gms_ref[rows, :]) * y_s5
        h = h_ref[rows, :] + _dot(m.astype(BF16), wo_ref[...])
        hn = _rms(h, g2_ref[...]).astype(BF16)
        h = h + 0.5 * _swiglu(hn, wg_ref, wu_ref, wd_ref)
        y_ref[rows, :] = _rms(h, gf_ref[...])


def _back(h, og, g5, gmg, gms, p, tm, sub):
    n = h.shape[0]
    row = lambda w: pl.BlockSpec((tm, w), lambda i: (i, 0))
    in_specs = [row(D_MODEL), row(GLA_VAL), row(S5_WIDTH), row(D_MODEL), row(D_MODEL),
                _const_spec((GLA_VAL, D_MODEL)), _const_spec((S5_WIDTH, S5_WIDTH)), _const_spec((S5_WIDTH, S5_WIDTH)),
                _const_spec((S5_WIDTH, D_MODEL)), _const_spec((D_MODEL, D_MODEL)), _const_spec((1, D_MODEL)),
                _const_spec((D_MODEL, D_FF)), _const_spec((D_MODEL, D_FF)), _const_spec((D_FF, D_MODEL)),
                _const_spec((1, D_MODEL))]
    return pl.pallas_call(
        functools.partial(_back_kernel, sub=sub),
        out_shape=jax.ShapeDtypeStruct((n, D_MODEL), F32),
        grid=(n // tm,),
        in_specs=in_specs,
        out_specs=row(D_MODEL),
        compiler_params=pltpu.CompilerParams(dimension_semantics=("parallel",), vmem_limit_bytes=VMEM_LIMIT),
        name="back",
    )(h, og, g5, gmg, gms, p['w_gla_out'], p['w_glu_a'], p['w_glu_b'], p['w_s5_out'], p['w_out'],
      p['g_ffn2'], p['wg2'], p['wu2'], p['wd2'], p['g_final'])


def _gla_state_in(s, nb):
    st = jnp.swapaxes(s.astype(F32), 2, 3)
    z = jnp.zeros_like(st)
    sel = (jnp.arange(GLA_HEADS) % 2 == 0)[None, :, None, None]
    st = jnp.where(sel, jnp.concatenate([st, z], axis=-1), jnp.concatenate([z, st], axis=-1))
    return st.reshape(nb, -1, GLA_DV, LANES)


def _gla_state_out(st, bsz):
    st = st.reshape(bsz, GLA_HEADS, GLA_DV, LANES)
    sel = (jnp.arange(GLA_HEADS) % 2 == 0)[None, :, None, None]
    return jnp.swapaxes(jnp.where(sel, st[..., :GLA_DK], st[..., GLA_DK:]), 2, 3)


def _s5_state_in(x0, nb):
    x = x0.astype(F32).reshape(x0.shape[0], S5_MODES, 2)
    return jnp.concatenate([x[..., 0], x[..., 1]], axis=-1).reshape(nb, -1, 2 * S5_MODES)


def _s5_state_out(x, bsz):
    x = x.reshape(bsz, 2, S5_MODES)
    return jnp.stack([x[:, 0], x[:, 1]], axis=-1).reshape(bsz, S5_GROUPS, S5_STATE, 2)


def _layer(x, s_gla, s_s5, p, tm, sub, tb, unroll, streams):
    bsz, t, _ = x.shape
    n = bsz * t
    nb = n // tb if streams else bsz
    h, q, k, v, r, u5, gmg, gms, la = _front(x.reshape(n, D_MODEL), p, tm, sub)
    og, g5, st, s5 = _mix(q, k, la, v, r, u5, _gla_state_in(s_gla, nb), _s5_state_in(s_s5, nb), p,
                          nb, tb, unroll, streams)
    y = _back(h, og, g5, gmg, gms, p, tm, sub)
    return y.reshape(bsz, t, D_MODEL), _gla_state_out(st, bsz), _s5_state_out(s5, bsz)


def kernel(x_prompt, x_sample, state_gla, state_s5, g_ffn1, w_ffn1_gate, w_ffn1_up, w_ffn1_down, g_mix, w_in, w_gate_up, b_gate, g_gla_head, w_gla_out, s5_lam_re, s5_lam_im, s5_log_dt, s5_b_re, s5_b_im, s5_c_re, s5_c_im, s5_d, w_glu_a, w_glu_b, w_s5_out, w_out, g_ffn2, w_ffn2_gate, w_ffn2_up, w_ffn2_down, g_final):
    vec = lambda a: a.astype(F32).reshape(1, -1)
    sizes = (GLA_KEY, GLA_KEY, GLA_VAL, GLA_VAL, GLA_GATE_RANK, S5_WIDTH, D_MODEL, D_MODEL)
    offs = [sum(sizes[:i]) for i in range(len(sizes) + 1)]
    wq, wk, wv, wr, wga, wu5, wgg, wgs = [w_in[:, offs[i]:offs[i + 1]] for i in range(len(sizes))]
    *s5_tiles, tab = _s5_prep(s5_lam_re, s5_lam_im, s5_log_dt, s5_b_re, s5_b_im, s5_c_re, s5_c_im)
    p = {
        'g_ffn1': vec(g_ffn1), 'wg1': w_ffn1_gate.astype(BF16), 'wu1': w_ffn1_up.astype(BF16),
        'wd1': w_ffn1_down.astype(BF16), 'g_mix': vec(g_mix),
        'w_main': jnp.concatenate([wq, wk, wv, wr, wu5, wgg, wgs], axis=1).astype(BF16),
        'w_ga': jnp.pad(wga, ((0, 0), (0, LANES - GLA_GATE_RANK))).astype(BF16),
        'w_gu': jnp.pad(w_gate_up, ((0, LANES - GLA_GATE_RANK), (0, 0))).astype(BF16),
        'b_gate': vec(b_gate), 'g_gla_head': vec(g_gla_head),
        's5_tiles': s5_tiles, 'tab': tab, 's5_d': vec(s5_d),
        'w_gla_out': w_gla_out.astype(BF16), 'w_glu_a': w_glu_a.astype(BF16), 'w_glu_b': w_glu_b.astype(BF16),
        'w_s5_out': w_s5_out.astype(BF16), 'w_out': w_out.astype(BF16),
        'g_ffn2': vec(g_ffn2), 'wg2': w_ffn2_gate.astype(BF16), 'wu2': w_ffn2_up.astype(BF16),
        'wd2': w_ffn2_down.astype(BF16), 'g_final': vec(g_final),
    }
    bp = x_prompt.shape[0]
    zero_gla = jnp.zeros((bp, GLA_HEADS, GLA_DK, GLA_DV), F32)
    zero_s5 = jnp.zeros((bp, S5_GROUPS, S5_STATE, 2), F32)
    y_p, gla_p, s5_p = _layer(x_prompt, zero_gla, zero_s5, p, tm=512, sub=256, tb=1024, unroll=4, streams=False)
    n_sample = x_sample.shape[0] * x_sample.shape[1]
    y_s, gla_s, s5_s = _layer(x_sample, state_gla, state_s5, p, tm=512, sub=256, tb=n_sample, unroll=4, streams=True)
    return (y_p, y_s, gla_p, s5_p, gla_s, s5_s)
```

```python
import functools
import math

import jax
import jax.numpy as jnp
from jax import lax
from jax.experimental import pallas as pl
from jax.experimental.pallas import tpu as pltpu

F32 = jnp.float32
BF16 = jnp.bfloat16

D_MODEL = 1024
D_FF = 2816
CHUNK = 64
SUB = 16
GLA_HEADS = 4
GLA_DK = 64
GLA_DV = 128
GLA_KEY = GLA_HEADS * GLA_DK
GLA_VAL = GLA_HEADS * GLA_DV
GLA_GATE_RANK = 16
GLA_GATE_TAU = 16.0
S5_WIDTH = 512
S5_GROUP = 16
S5_GROUPS = 32
S5_STATE = 64
S5_MODES = S5_GROUPS * S5_STATE
EPS = 1e-6
SUBLANES = 8
LANES = 128
MXU_TILE = 256
S5_STEP = SUBLANES
S5_PAIRS = S5_GROUPS // 2
PAIR_CH = 2 * S5_GROUP
PAIR_MODES = 2 * S5_STATE
S5_SLABS = S5_WIDTH // LANES
N_SCAN_TABLES = 8
EXP_CLAMP = 80.0
FF_CHUNK = 512
VMEM_LIMIT = 60 * 1024 * 1024

_OQ, _OK, _OV, _OR, _OU, _OGG, _OGS = 0, 256, 512, 1024, 1536, 2048, 3072
W_MAIN_COLS = 4096


def _dot(a, b):
    return jnp.dot(a, b, preferred_element_type=F32)


def _dot_nt(a, b):
    return lax.dot_general(a, b, (((1,), (1,)), ((), ())), preferred_element_type=F32)


def _split_dot(a, b):
    a1 = a.astype(BF16)
    a2 = (a - a1.astype(F32)).astype(BF16)
    b1 = b.astype(BF16)
    b2 = (b - b1.astype(F32)).astype(BF16)
    return _dot(a1, b1) + _dot(a1, b2) + _dot(a2, b1)


def _cmul(ar, ai, br, bi):
    return ar * br - ai * bi, ar * bi + ai * br


def _sigmoid(x):
    return 1.0 / (1.0 + jnp.exp(-x))


def _rms(x, g):
    ms = jnp.mean(x * x, axis=-1, keepdims=True)
    return x * lax.rsqrt(ms + EPS) * g


def _ff_chunks():
    out, c0 = [], 0
    while c0 < D_FF:
        n = min(FF_CHUNK, D_FF - c0)
        out.append((c0, n))
        c0 += n
    return out


def _swiglu(xn, wg_ref, wu_ref, wd_ref):
    acc = None
    for c0, n in _ff_chunks():
        g = _dot(xn, wg_ref[:, c0:c0 + n])
        u = _dot(xn, wu_ref[:, c0:c0 + n])
        a = (g * _sigmoid(g) * u).astype(BF16)
        part = _dot(a, wd_ref[c0:c0 + n, :])
        acc = part if acc is None else acc + part
    return acc


def _const_spec(shape):
    nd = len(shape)
    return pl.BlockSpec(shape, lambda *_: (0,) * nd, pipeline_mode=pl.Buffered(1))


def _front_kernel(x_ref, g1_ref, wg_ref, wu_ref, wd_ref, gmix_ref, win_ref, wga_ref, wgu_ref, bg_ref,
                  h_ref, q_ref, k_ref, v_ref, r_ref, u5_ref, gmg_ref, gms_ref, la_ref, *, sub):
    for r0 in range(0, x_ref.shape[0], sub):
        rows = slice(r0, r0 + sub)
        x = x_ref[rows, :]
        xn = _rms(x, g1_ref[...]).astype(BF16)
        h = x + 0.5 * _swiglu(xn, wg_ref, wu_ref, wd_ref)
        h_ref[rows, :] = h
        un = _rms(h, gmix_ref[...]).astype(BF16)

        def proj(c0, n):
            return _dot(un, win_ref[:, c0:c0 + n])

        q_ref[rows, :] = proj(_OQ, GLA_KEY) * (GLA_DK ** -0.5)
        k_ref[rows, :] = proj(_OK, GLA_KEY)
        v_ref[rows, :] = proj(_OV, GLA_VAL).astype(BF16)
        r_ref[rows, :] = proj(_OR, GLA_VAL)
        u5 = proj(_OU, S5_WIDTH)
        for c in range(S5_SLABS):
            u5_ref[c, rows, :] = u5[:, LANES * c:LANES * (c + 1)]
        gmg_ref[rows, :] = proj(_OGG, D_MODEL)
        gms_ref[rows, :] = proj(_OGS, D_MODEL)
        ga = _dot(un, wga_ref[...])
        pre = _dot(ga.astype(BF16), wgu_ref[...]) + bg_ref[...]
        log_sig = jnp.minimum(pre, 0.0) - jnp.log(1.0 + jnp.exp(-jnp.abs(pre)))
        la_ref[rows, :] = log_sig * (1.0 / GLA_GATE_TAU)


def _front(x2d, p, tm, sub):
    n = x2d.shape[0]
    row = lambda w: pl.BlockSpec((tm, w), lambda i: (i, 0))
    slab = pl.BlockSpec((S5_SLABS, tm, LANES), lambda i: (0, i, 0))
    in_specs = [row(D_MODEL), _const_spec((1, D_MODEL)),
                _const_spec((D_MODEL, D_FF)), _const_spec((D_MODEL, D_FF)), _const_spec((D_FF, D_MODEL)),
                _const_spec((1, D_MODEL)), _const_spec((D_MODEL, W_MAIN_COLS)),
                _const_spec((D_MODEL, LANES)), _const_spec((LANES, GLA_KEY)), _const_spec((1, GLA_KEY))]
    sds = lambda w, dt: jax.ShapeDtypeStruct((n, w), dt)
    out_shape = [sds(D_MODEL, F32), sds(GLA_KEY, F32), sds(GLA_KEY, F32), sds(GLA_VAL, BF16), sds(GLA_VAL, F32),
                 jax.ShapeDtypeStruct((S5_SLABS, n, LANES), F32), sds(D_MODEL, F32), sds(D_MODEL, F32),
                 sds(GLA_KEY, F32)]
    out_specs = [row(D_MODEL), row(GLA_KEY), row(GLA_KEY), row(GLA_VAL), row(GLA_VAL), slab,
                 row(D_MODEL), row(D_MODEL), row(GLA_KEY)]
    return pl.pallas_call(
        functools.partial(_front_kernel, sub=sub),
        out_shape=out_shape,
        grid=(n // tm,),
        in_specs=in_specs,
        out_specs=out_specs,
        compiler_params=pltpu.CompilerParams(dimension_semantics=("parallel",), vmem_limit_bytes=VMEM_LIMIT),
        name="front",
    )(x2d, p['g_ffn1'], p['wg1'], p['wu1'], p['wd1'], p['g_mix'], p['w_main'], p['w_ga'], p['w_gu'], p['b_gate'])


def _shift_lanes_256(x, s):
    lane = lax.broadcasted_iota(jnp.int32, (1, LANES), 1)
    lo, hi = x[:, :LANES], x[:, LANES:]
    if s == 0:
        return x
    if s >= LANES:
        r = s - LANES
        moved = lo if r == 0 else jnp.where(lane >= r, pltpu.roll(lo, r, axis=1), 0.0)
        return jnp.concatenate([jnp.zeros_like(lo), moved], axis=1)
    rl, rh = pltpu.roll(lo, s, axis=1), pltpu.roll(hi, s, axis=1)
    return jnp.concatenate([jnp.where(lane >= s, rl, 0.0), jnp.where(lane >= s, rh, rl)], axis=1)


def _s5_prep_kernel(lrc_ref, lic_ref, ldtc_ref, lrr_ref, lir_ref, ldtr_ref, br_ref, bi_ref, cr_ref, ci_ref,
                    w1r_ref, w1i_ref, w3xr_ref, w3xi_ref, w3u_ref, tab_ref):
    lr, li = lrc_ref[...], lic_ref[...]
    dt = jnp.exp(ldtc_ref[...])
    ar, th = lr * dt, li * dt

    def powers(kvec):
        mag = jnp.exp(ar * kvec)
        ang = th * kvec
        return mag * jnp.cos(ang), mag * jnp.sin(ang)

    lane = lax.broadcasted_iota(jnp.int32, (1, MXU_TILE), 1)
    step = lax.shift_right_logical(lane, 5)
    lane_group = lax.shift_right_logical(lane, 4) & 1
    mode = lax.broadcasted_iota(jnp.int32, (S5_MODES, 1), 0)
    own = lane_group == (lax.shift_right_logical(mode, 6) & 1)
    chan = lax.broadcasted_iota(jnp.int32, (S5_GROUP, 1), 0)
    rep = jnp.where((lane & (S5_GROUP - 1)) == chan, 1.0, 0.0)
    b_r, b_i = _split_dot(br_ref[...], rep), _split_dot(bi_ref[...], rep)
    c_r, c_i = _split_dot(cr_ref[...], rep), _split_dot(ci_ref[...], rep)

    ab_re, ab_im = powers(jnp.ones((1, 1), F32))
    nr, ni = ab_re - 1.0, ab_im
    den = lr * lr + li * li
    f_re = (nr * lr + ni * li) / den
    f_im = (ni * lr - nr * li) / den
    bbr, bbi = _cmul(f_re, f_im, b_r, b_i)

    p7r, p7i = powers((S5_STEP - 1 - step).astype(F32))
    w1r, w1i = _cmul(p7r, p7i, bbr, bbi)
    w1r_ref[...] = jnp.where(own, w1r, 0.0).astype(BF16)
    w1i_ref[...] = jnp.where(own, w1i, 0.0).astype(BF16)
    p1r, p1i = powers((step + 1).astype(F32))
    xr, xi = _cmul(p1r, p1i, c_r, c_i)
    w3xr_ref[...] = jnp.where(own, xr, 0.0).astype(BF16)
    w3xi_ref[...] = jnp.where(own, -xi, 0.0).astype(BF16)

    p0r, p0i = powers(step.astype(F32))
    car, cai = _cmul(p0r, p0i, c_r, c_i)
    car, cai = jnp.where(own, car, 0.0), jnp.where(own, cai, 0.0)
    lane_w = lax.broadcasted_iota(jnp.int32, (1, S5_WIDTH), 1)
    rep_w = jnp.where((lane_w & (S5_GROUP - 1)) == chan, 1.0, 0.0)
    own_w = lax.shift_right_logical(lane_w, 4) == lax.shift_right_logical(mode, 6)
    dbr, dbi = _cmul(f_re, f_im, _split_dot(br_ref[...], rep_w), _split_dot(bi_ref[...], rep_w))
    dbr, dbi = jnp.where(own_w, dbr, 0.0), jnp.where(own_w, dbi, 0.0)
    taps = _split_dot(dbr.T, car) - _split_dot(dbi.T, cai)
    for i in range(S5_STEP):
        blk = _shift_lanes_256(taps, PAIR_CH * i)
        w3u_ref[:, PAIR_CH * i:PAIR_CH * (i + 1), :] = blk.reshape(S5_PAIRS, PAIR_CH, MXU_TILE).astype(BF16)

    arr = lrr_ref[...] * jnp.exp(ldtr_ref[...])
    thr = lir_ref[...] * jnp.exp(ldtr_ref[...])
    pos = lax.broadcasted_iota(jnp.int32, (SUBLANES, 1), 0)

    def powers_rows(kcol):
        mag = jnp.exp(arr * kcol)
        ang = thr * kcol
        return mag * jnp.cos(ang), mag * jnp.sin(ang)

    for s, kstep in enumerate((1, 2, 4)):
        kr, ki = powers_rows(jnp.full((1, 1), float(S5_STEP * kstep), F32))
        tab_ref[2 * s] = jnp.where(pos >= kstep, kr, 0.0)
        tab_ref[2 * s + 1] = jnp.where(pos >= kstep, ki, 0.0)
    tab_ref[6], tab_ref[7] = powers_rows(((pos + 1) * S5_STEP).astype(F32))


def _s5_prep(lam_re, lam_im, log_dt, b_re, b_im, c_re, c_im):
    m = S5_MODES
    ldt = jnp.broadcast_to(log_dt.astype(F32)[:, None], (S5_GROUPS, S5_STATE))
    col = lambda a: a.astype(F32).reshape(m, 1)
    rowv = lambda a: a.astype(F32).reshape(1, m)
    bmat = lambda a: a.astype(F32).reshape(m, S5_GROUP)
    cmat = lambda a: a.astype(F32).transpose(0, 2, 1).reshape(m, S5_GROUP)
    tile = jax.ShapeDtypeStruct((m, MXU_TILE), BF16)
    out_shape = [tile] * 4 + [jax.ShapeDtypeStruct((S5_PAIRS, MXU_TILE, MXU_TILE), BF16),
                              jax.ShapeDtypeStruct((N_SCAN_TABLES, SUBLANES, m), F32)]
    w1r, w1i, w3xr, w3xi, w3u, tab = pl.pallas_call(
        _s5_prep_kernel,
        out_shape=out_shape,
        compiler_params=pltpu.CompilerParams(vmem_limit_bytes=VMEM_LIMIT),
        name="s5_prep",
    )(col(lam_re), col(lam_im), col(ldt), rowv(lam_re), rowv(lam_im), rowv(ldt),
      bmat(b_re), bmat(b_im), cmat(c_re), cmat(c_im))
    per_pair = lambda a: a.reshape(S5_PAIRS, PAIR_MODES, MXU_TILE)
    return per_pair(w1r), per_pair(w1i), per_pair(w3xr), per_pair(w3xi), w3u, tab


def _lane_window(lo):
    lane = lax.broadcasted_iota(jnp.int32, (1, LANES), 1)
    return (lane >= lo) & (lane < lo + PAIR_CH)


def _merge_windows(pieces):
    acc = pieces[0]
    for k in range(1, len(pieces)):
        acc = jnp.where(_lane_window(PAIR_CH * k), pieces[k], acc)
    return acc


def _rolled(cache, key, make, shift):
    if (key, shift) not in cache:
        x = make()
        cache[(key, shift)] = x if shift == 0 else pltpu.roll(x, shift, axis=1)
    return cache[(key, shift)]


def _s5_regroup_in(step_slab):
    per_tile = LANES // PAIR_CH
    cache, outs = {}, []
    for q in range(S5_PAIRS):
        halves = []
        for m in range(MXU_TILE // LANES):
            pieces = []
            for i in range(per_tile * m, per_tile * (m + 1)):
                c = q // per_tile
                pieces.append(_rolled(cache, (i, c), lambda i=i, c=c: step_slab(i, c),
                                      PAIR_CH * ((i - q) % per_tile)))
            halves.append(_merge_windows(pieces))
        outs.append(jnp.concatenate(halves, axis=1))
    return outs


def _s5_regroup_out(ys):
    per_tile = LANES // PAIR_CH
    cache, out = {}, []
    for j in range(S5_STEP):
        slabs = []
        for c in range(S5_SLABS):
            pieces = []
            for q in range(per_tile * c, per_tile * (c + 1)):
                m = j // per_tile
                pieces.append(_rolled(cache, (q, m), lambda q=q, m=m: ys[q][:, LANES * m:LANES * (m + 1)],
                                      PAIR_CH * ((q - j) % per_tile)))
            slabs.append(_merge_windows(pieces))
        out.append(slabs)
    return out


def _mix_kernel(q_ref, k_ref, la_ref, v_ref, r_ref, u5_ref, st0_ref, s50_ref, gh_ref,
                w1r_ref, w1i_ref, w3xr_ref, w3xi_ref, w3u_ref, d_ref, tab_ref,
                og_ref, g5_ref, sto_ref, s5o_ref,
                st_sc, car_sc, o_sc, g5_sc, *, tb, unroll, streams):
    nm = S5_MODES

    if not streams:
        @pl.when(pl.program_id(1) == 0)
        def _():
            st_sc[...] = st0_ref[0]
            car_sc[...] = jnp.broadcast_to(s50_ref[0], (SUBLANES, 2 * nm))

    ti = lax.broadcasted_iota(jnp.int32, (CHUNK, CHUNK), 0)
    si = lax.broadcasted_iota(jnp.int32, (CHUNK, CHUNK), 1)
    causal = si <= ti
    tril = jnp.where(causal, 1.0, 0.0).astype(BF16)
    lane = lax.broadcasted_iota(jnp.int32, (1, GLA_KEY), 1) % LANES
    head_mask = (lane < GLA_DK, lane >= GLA_DK)
    nsub = CHUNK // SUB

    def chunk(c):
        rows = pl.ds(c * CHUNK if isinstance(c, int) else pl.multiple_of(c * CHUNK, CHUNK), CHUNK)
        la = la_ref[rows, :]
        hi = la.astype(BF16)
        lo = (la - hi.astype(F32)).astype(BF16)
        b = _dot(tril, hi) + _dot(tril, lo)
        refs = [jnp.zeros((1, GLA_KEY), F32)] + [b[SUB * i - 1:SUB * i, :] for i in range(1, nsub)]
        rb = jnp.concatenate([jnp.broadcast_to(x, (SUB, GLA_KEY)) for x in refs], axis=0)
        q, k = q_ref[rows, :], k_ref[rows, :]
        b_last = b[CHUNK - 1:CHUNK, :]
        qt = q * jnp.exp(b - rb)
        qd = q * jnp.exp(b)
        qt = [jnp.where(m, qt, 0.0).astype(BF16) for m in head_mask]
        qd = [jnp.where(m, qd, 0.0).astype(BF16) for m in head_mask]
        kd = (k * jnp.exp(b_last - b)).astype(BF16)
        e_last = jnp.exp(b_last)
        kts = [(k * jnp.exp(jnp.minimum(x - b, EXP_CLAMP))).astype(BF16) for x in refs]
        for h in range(GLA_HEADS):
            pc = slice(LANES * (h // 2), LANES * (h // 2 + 1))
            qt_h = qt[h % 2][:, pc]
            sc = jnp.concatenate(
                [_dot_nt(qt_h[SUB * i:SUB * (i + 1)], kts[i][:, pc]) for i in range(nsub)], axis=0)
            pm = jnp.where(causal, sc, 0.0).astype(BF16)
            vh = v_ref[rows, GLA_DV * h:GLA_DV * (h + 1)]
            st = st0_ref[0, c * GLA_HEADS + h] if streams else st_sc[h]
            o = _dot(pm, vh) + _dot_nt(qd[h % 2][:, pc], st.astype(BF16))
            o_sc[rows, GLA_DV * h:GLA_DV * (h + 1)] = o
            vt = vh.astype(F32).T.astype(BF16)
            st_new = st * e_last[:, pc] + _dot(vt, kd[:, pc])
            if streams:
                sto_ref[0, c * GLA_HEADS + h] = st_new
            else:
                st_sc[h] = st_new

    nchunks = tb // CHUNK
    if nchunks <= unroll:
        for c in range(nchunks):
            chunk(c)
    else:
        def body(i, carry):
            for j in range(unroll):
                chunk(i * unroll + j)
            return carry
        lax.fori_loop(0, nchunks // unroll, body, 0)

    o_all = o_sc[...]
    normed = []
    for h in range(GLA_HEADS):
        oh = o_all[:, GLA_DV * h:GLA_DV * (h + 1)]
        normed.append(oh * lax.rsqrt(jnp.mean(oh * oh, axis=-1, keepdims=True) + EPS))
    r = r_ref[...]
    og = jnp.concatenate(normed, axis=1) * gh_ref[...] * (r * _sigmoid(r))
    og_ref[...] = og.astype(BF16)
    if not streams:
        sto_ref[0] = st_sc[...]

    tg = tb // S5_STEP
    ng = tg // SUBLANES
    step_slab = lambda i, c: u5_ref[c, pl.ds(i, tg, stride=S5_STEP), :]
    uq = [x.astype(BF16) for x in _s5_regroup_in(step_slab)]
    w = [_dot_nt(uq[q], jnp.concatenate([w1r_ref[q], w1i_ref[q]], axis=0)) for q in range(S5_PAIRS)]
    xr = jnp.concatenate([x[:, :LANES] for x in w], axis=1).reshape(ng, SUBLANES, nm)
    xi = jnp.concatenate([x[:, LANES:] for x in w], axis=1).reshape(ng, SUBLANES, nm)
    for s, kstep in enumerate((1, 2, 4)):
        ar, ai = tab_ref[2 * s], tab_ref[2 * s + 1]
        sr = pltpu.roll(xr, kstep, axis=1)
        sim = pltpu.roll(xi, kstep, axis=1)
        xr, xi = xr + ar * sr - ai * sim, xi + ar * sim + ai * sr
    pr, pi = tab_ref[6], tab_ref[7]
    first_row = lax.broadcasted_iota(jnp.int32, (SUBLANES, nm), 0) == 0
    if not streams:
        cr, ci = car_sc[:, :nm], car_sc[:, nm:]
    prev_r, prev_i = [], []
    for g in range(ng):
        if streams:
            cr = jnp.broadcast_to(s50_ref[0, g:g + 1, :nm], (SUBLANES, nm))
            ci = jnp.broadcast_to(s50_ref[0, g:g + 1, nm:], (SUBLANES, nm))
        yr = xr[g] + pr * cr - pi * ci
        yi = xi[g] + pr * ci + pi * cr
        prev_r.append(jnp.where(first_row, cr, pltpu.roll(yr, 1, axis=0)))
        prev_i.append(jnp.where(first_row, ci, pltpu.roll(yi, 1, axis=0)))
        if streams:
            s5o_ref[0, g:g + 1, :nm] = yr[SUBLANES - 1:SUBLANES]
            s5o_ref[0, g:g + 1, nm:] = yi[SUBLANES - 1:SUBLANES]
        else:
            cr = jnp.broadcast_to(yr[SUBLANES - 1:SUBLANES], (SUBLANES, nm))
            ci = jnp.broadcast_to(yi[SUBLANES - 1:SUBLANES], (SUBLANES, nm))
    if not streams:
        car_sc[:, :nm] = cr
        car_sc[:, nm:] = ci
        s5o_ref[0] = car_sc[0:1, :]
    prev_r = jnp.concatenate(prev_r, axis=0).astype(BF16)
    prev_i = jnp.concatenate(prev_i, axis=0).astype(BF16)
    ys = []
    for q in range(S5_PAIRS):
        ms = slice(PAIR_MODES * q, PAIR_MODES * (q + 1))
        lhs = jnp.concatenate([prev_r[:, ms], prev_i[:, ms], uq[q]], axis=1)
        ys.append(_dot(lhs, jnp.concatenate([w3xr_ref[q], w3xi_ref[q], w3u_ref[q]], axis=0)))
    conv = _s5_regroup_out(ys)
    for j in range(S5_STEP):
        for c in range(S5_SLABS):
            y = conv[j][c] + d_ref[:, LANES * c:LANES * (c + 1)] * step_slab(j, c)
            inner = math.sqrt(2.0 / math.pi) * (y + 0.044715 * (y * y * y))
            g5_sc[c, pl.ds(j, tg, stride=S5_STEP), :] = 0.5 * y * (1.0 + jnp.tanh(inner))
    g5_ref[...] = jnp.concatenate([g5_sc[c] for c in range(S5_SLABS)], axis=1).astype(BF16)


def _mix(q, k, la, v, r, u5, st0, s50, p, nb, tb, unroll, streams):
    n = q.shape[0]
    nt = n // (nb * tb)
    nst = st0.shape[1]
    nseq = s50.shape[1]
    row = lambda w: pl.BlockSpec((tb, w), lambda b, t: (b * nt + t, 0))
    slab = pl.BlockSpec((S5_SLABS, tb, LANES), lambda b, t: (0, b * nt + t, 0))
    st_spec = pl.BlockSpec((1, nst, GLA_DV, LANES), lambda b, t: (b, 0, 0, 0))
    s5_spec = pl.BlockSpec((1, nseq, 2 * S5_MODES), lambda b, t: (b, 0, 0))
    pair_tile = _const_spec((S5_PAIRS, PAIR_MODES, MXU_TILE))
    in_specs = [row(GLA_KEY), row(GLA_KEY), row(GLA_KEY), row(GLA_VAL), row(GLA_VAL), slab,
                st_spec, s5_spec, _const_spec((1, GLA_VAL)),
                pair_tile, pair_tile, pair_tile, pair_tile, _const_spec((S5_PAIRS, MXU_TILE, MXU_TILE)),
                _const_spec((1, S5_WIDTH)), _const_spec((N_SCAN_TABLES, SUBLANES, S5_MODES))]
    out_shape = [jax.ShapeDtypeStruct((n, GLA_VAL), BF16), jax.ShapeDtypeStruct((n, S5_WIDTH), BF16),
                 jax.ShapeDtypeStruct(st0.shape, F32), jax.ShapeDtypeStruct(s50.shape, F32)]
    return pl.pallas_call(
        functools.partial(_mix_kernel, tb=tb, unroll=unroll, streams=streams),
        out_shape=out_shape,
        grid=(nb, nt),
        in_specs=in_specs,
        out_specs=[row(GLA_VAL), row(S5_WIDTH), st_spec, s5_spec],
        scratch_shapes=[pltpu.VMEM((GLA_HEADS, GLA_DV, LANES), F32),
                        pltpu.VMEM((SUBLANES, 2 * S5_MODES), F32),
                        pltpu.VMEM((tb, GLA_VAL), F32),
                        pltpu.VMEM((S5_SLABS, tb, LANES), F32)],
        compiler_params=pltpu.CompilerParams(dimension_semantics=("arbitrary", "arbitrary"),
                                             vmem_limit_bytes=VMEM_LIMIT),
        name="mix",
    )(q, k, la, v, r, u5, st0, s50, p['g_gla_head'], *p['s5_tiles'], p['s5_d'], p['tab'])


def _back_kernel(h_ref, og_ref, g5_ref, gmg_ref, gms_ref, wgo_ref, wa_ref, wb_ref, wso_ref, wo_ref,
                 g2_ref, wg_ref, wu_ref, wd_ref, gf_ref, y_ref, *, sub):
    for r0 in range(0, h_ref.shape[0], sub):
        rows = slice(r0, r0 + sub)
        y_gla = _dot(og_ref[rows, :], wgo_ref[...])
        g5 = g5_ref[rows, :]
        glu = _dot(g5, wa_ref[...]) * _sigmoid(_dot(g5, wb_ref[...]))
        y_s5 = _dot(glu.astype(BF16), wso_ref[...])
        m = _sigmoid(gmg_ref[rows, :]) * y_gla + _sigmoid(gms_ref[rows, :]) * y_s5
        h = h_ref[rows, :] + _dot(m.astype(BF16), wo_ref[...])
        hn = _rms(h, g2_ref[...]).astype(BF16)
        h = h + 0.5 * _swiglu(hn, wg_ref, wu_ref, wd_ref)
        y_ref[rows, :] = _rms(h, gf_ref[...])


def _back(h, og, g5, gmg, gms, p, tm, sub):
    n = h.shape[0]
    row = lambda w: pl.BlockSpec((tm, w), lambda i: (i, 0))
    in_specs = [row(D_MODEL), row(GLA_VAL), row(S5_WIDTH), row(D_MODEL), row(D_MODEL),
                _const_spec((GLA_VAL, D_MODEL)), _const_spec((S5_WIDTH, S5_WIDTH)), _const_spec((S5_WIDTH, S5_WIDTH)),
                _const_spec((S5_WIDTH, D_MODEL)), _const_spec((D_MODEL, D_MODEL)), _const_spec((1, D_MODEL)),
                _const_spec((D_MODEL, D_FF)), _const_spec((D_MODEL, D_FF)), _const_spec((D_FF, D_MODEL)),
                _const_spec((1, D_MODEL))]
    return pl.pallas_call(
        functools.partial(_back_kernel, sub=sub),
        out_shape=jax.ShapeDtypeStruct((n, D_MODEL), F32),
        grid=(n // tm,),
        in_specs=in_specs,
        out_specs=row(D_MODEL),
        compiler_params=pltpu.CompilerParams(dimension_semantics=("parallel",), vmem_limit_bytes=VMEM_LIMIT),
        name="back",
    )(h, og, g5, gmg, gms, p['w_gla_out'], p['w_glu_a'], p['w_glu_b'], p['w_s5_out'], p['w_out'],
      p['g_ffn2'], p['wg2'], p['wu2'], p['wd2'], p['g_final'])


def _gla_state_in(s, nb):
    st = jnp.swapaxes(s.astype(F32), 2, 3)
    z = jnp.zeros_like(st)
    sel = (jnp.arange(GLA_HEADS) % 2 == 0)[None, :, None, None]
    st = jnp.where(sel, jnp.concatenate([st, z], axis=-1), jnp.concatenate([z, st], axis=-1))
    return st.reshape(nb, -1, GLA_DV, LANES)


def _gla_state_out(st, bsz):
    st = st.reshape(bsz, GLA_HEADS, GLA_DV, LANES)
    sel = (jnp.arange(GLA_HEADS) % 2 == 0)[None, :, None, None]
    return jnp.swapaxes(jnp.where(sel, st[..., :GLA_DK], st[..., GLA_DK:]), 2, 3)


def _s5_state_in(x0, nb):
    x = x0.astype(F32).reshape(x0.shape[0], S5_MODES, 2)
    return jnp.concatenate([x[..., 0], x[..., 1]], axis=-1).reshape(nb, -1, 2 * S5_MODES)


def _s5_state_out(x, bsz):
    x = x.reshape(bsz, 2, S5_MODES)
    return jnp.stack([x[:, 0], x[:, 1]], axis=-1).reshape(bsz, S5_GROUPS, S5_STATE, 2)


def _layer(x, s_gla, s_s5, p, tm, sub, tb, unroll, streams):
    bsz, t, _ = x.shape
    n = bsz * t
    nb = n // tb if streams else bsz
    h, q, k, v, r, u5, gmg, gms, la = _front(x.reshape(n, D_MODEL), p, tm, sub)
    og, g5, st, s5 = _mix(q, k, la, v, r, u5, _gla_state_in(s_gla, nb), _s5_state_in(s_s5, nb), p,
                          nb, tb, unroll, streams)
    y = _back(h, og, g5, gmg, gms, p, tm, sub)
    return y.reshape(bsz, t, D_MODEL), _gla_state_out(st, bsz), _s5_state_out(s5, bsz)


def kernel(x_prompt, x_sample, state_gla, state_s5, g_ffn1, w_ffn1_gate, w_ffn1_up, w_ffn1_down, g_mix, w_in, w_gate_up, b_gate, g_gla_head, w_gla_out, s5_lam_re, s5_lam_im, s5_log_dt, s5_b_re, s5_b_im, s5_c_re, s5_c_im, s5_d, w_glu_a, w_glu_b, w_s5_out, w_out, g_ffn2, w_ffn2_gate, w_ffn2_up, w_ffn2_down, g_final):
    vec = lambda a: a.astype(F32).reshape(1, -1)
    sizes = (GLA_KEY, GLA_KEY, GLA_VAL, GLA_VAL, GLA_GATE_RANK, S5_WIDTH, D_MODEL, D_MODEL)
    offs = [sum(sizes[:i]) for i in range(len(sizes) + 1)]
    wq, wk, wv, wr, wga, wu5, wgg, wgs = [w_in[:, offs[i]:offs[i + 1]] for i in range(len(sizes))]
    *s5_tiles, tab = _s5_prep(s5_lam_re, s5_lam_im, s5_log_dt, s5_b_re, s5_b_im, s5_c_re, s5_c_im)
    p = {
        'g_ffn1': vec(g_ffn1), 'wg1': w_ffn1_gate.astype(BF16), 'wu1': w_ffn1_up.astype(BF16),
        'wd1': w_ffn1_down.astype(BF16), 'g_mix': vec(g_mix),
        'w_main': jnp.concatenate([wq, wk, wv, wr, wu5, wgg, wgs], axis=1).astype(BF16),
        'w_ga': jnp.pad(wga, ((0, 0), (0, LANES - GLA_GATE_RANK))).astype(BF16),
        'w_gu': jnp.pad(w_gate_up, ((0, LANES - GLA_GATE_RANK), (0, 0))).astype(BF16),
        'b_gate': vec(b_gate), 'g_gla_head': vec(g_gla_head),
        's5_tiles': s5_tiles, 'tab': tab, 's5_d': vec(s5_d),
        'w_gla_out': w_gla_out.astype(BF16), 'w_glu_a': w_glu_a.astype(BF16), 'w_glu_b': w_glu_b.astype(BF16),
        'w_s5_out': w_s5_out.astype(BF16), 'w_out': w_out.astype(BF16),
        'g_ffn2': vec(g_ffn2), 'wg2': w_ffn2_gate.astype(BF16), 'wu2': w_ffn2_up.astype(BF16),
        'wd2': w_ffn2_down.astype(BF16), 'g_final': vec(g_final),
    }
    bp = x_prompt.shape[0]
    zero_gla = jnp.zeros((bp, GLA_HEADS, GLA_DK, GLA_DV), F32)
    zero_s5 = jnp.zeros((bp, S5_GROUPS, S5_STATE, 2), F32)
    y_p, gla_p, s5_p = _layer(x_prompt, zero_gla, zero_s5, p, tm=512, sub=256, tb=1024, unroll=16, streams=False)
    n_sample = x_sample.shape[0] * x_sample.shape[1]
    y_s, gla_s, s5_s = _layer(x_sample, state_gla, state_s5, p, tm=512, sub=256, tb=n_sample, unroll=4, streams=True)
    return (y_p, y_s, gla_p, s5_p, gla_s, s5_s)
```

```python
import functools
import math

import jax
import jax.numpy as jnp
from jax import lax
from jax.experimental import pallas as pl
from jax.experimental.pallas import tpu as pltpu

F32 = jnp.float32
BF16 = jnp.bfloat16

D_MODEL = 1024
D_FF = 2816
CHUNK = 64
SUB = 16
GLA_HEADS = 4
GLA_DK = 64
GLA_DV = 128
GLA_KEY = GLA_HEADS * GLA_DK
GLA_VAL = GLA_HEADS * GLA_DV
GLA_GATE_RANK = 16
GLA_GATE_TAU = 16.0
S5_WIDTH = 512
S5_GROUP = 16
S5_GROUPS = 32
S5_STATE = 64
S5_MODES = S5_GROUPS * S5_STATE
EPS = 1e-6
SUBLANES = 8
LANES = 128
MXU_TILE = 256
S5_STEP = SUBLANES
S5_PAIRS = S5_GROUPS // 2
PAIR_CH = 2 * S5_GROUP
PAIR_MODES = 2 * S5_STATE
S5_SLABS = S5_WIDTH // LANES
N_SCAN_TABLES = 8
EXP_CLAMP = 80.0
FF_CHUNK = 512
VMEM_LIMIT = 60 * 1024 * 1024

_OQ, _OK, _OV, _OR, _OU, _OGG, _OGS = 0, 256, 512, 1024, 1536, 2048, 3072
W_MAIN_COLS = 4096


def _dot(a, b):
    return jnp.dot(a, b, preferred_element_type=F32)


def _dot_nt(a, b):
    return lax.dot_general(a, b, (((1,), (1,)), ((), ())), preferred_element_type=F32)


def _split_dot(a, b):
    a1 = a.astype(BF16)
    a2 = (a - a1.astype(F32)).astype(BF16)
    b1 = b.astype(BF16)
    b2 = (b - b1.astype(F32)).astype(BF16)
    return _dot(a1, b1) + _dot(a1, b2) + _dot(a2, b1)


def _cmul(ar, ai, br, bi):
    return ar * br - ai * bi, ar * bi + ai * br


def _sigmoid(x):
    return 1.0 / (1.0 + jnp.exp(-x))


def _rms(x, g):
    ms = jnp.mean(x * x, axis=-1, keepdims=True)
    return x * lax.rsqrt(ms + EPS) * g


def _ff_chunks():
    out, c0 = [], 0
    while c0 < D_FF:
        n = min(FF_CHUNK, D_FF - c0)
        out.append((c0, n))
        c0 += n
    return out


def _swiglu(xn, wg_ref, wu_ref, wd_ref):
    acc = None
    for c0, n in _ff_chunks():
        g = _dot(xn, wg_ref[:, c0:c0 + n])
        u = _dot(xn, wu_ref[:, c0:c0 + n])
        a = (g * _sigmoid(g) * u).astype(BF16)
        part = _dot(a, wd_ref[c0:c0 + n, :])
        acc = part if acc is None else acc + part
    return acc


def _const_spec(shape):
    nd = len(shape)
    return pl.BlockSpec(shape, lambda *_: (0,) * nd, pipeline_mode=pl.Buffered(1))


def _front_rows(x, g1_ref, wg_ref, wu_ref, wd_ref, gmix_ref, win_ref, wga_ref, wgu_ref, bg_ref):
    xn = _rms(x, g1_ref[...]).astype(BF16)
    h = x + 0.5 * _swiglu(xn, wg_ref, wu_ref, wd_ref)
    un = _rms(h, gmix_ref[...]).astype(BF16)

    def proj(c0, n):
        return _dot(un, win_ref[:, c0:c0 + n])

    ga = _dot(un, wga_ref[...])
    pre = _dot(ga.astype(BF16), wgu_ref[...]) + bg_ref[...]
    log_sig = jnp.minimum(pre, 0.0) - jnp.log(1.0 + jnp.exp(-jnp.abs(pre)))
    return dict(h=h, q=proj(_OQ, GLA_KEY) * (GLA_DK ** -0.5), k=proj(_OK, GLA_KEY), v=proj(_OV, GLA_VAL),
                r=proj(_OR, GLA_VAL), u5=proj(_OU, S5_WIDTH), gmg=proj(_OGG, D_MODEL), gms=proj(_OGS, D_MODEL),
                la=log_sig * (1.0 / GLA_GATE_TAU))


def _shift_lanes_256(x, s):
    lane = lax.broadcasted_iota(jnp.int32, (1, LANES), 1)
    lo, hi = x[:, :LANES], x[:, LANES:]
    if s == 0:
        return x
    if s >= LANES:
        r = s - LANES
        moved = lo if r == 0 else jnp.where(lane >= r, pltpu.roll(lo, r, axis=1), 0.0)
        return jnp.concatenate([jnp.zeros_like(lo), moved], axis=1)
    rl, rh = pltpu.roll(lo, s, axis=1), pltpu.roll(hi, s, axis=1)
    return jnp.concatenate([jnp.where(lane >= s, rl, 0.0), jnp.where(lane >= s, rh, rl)], axis=1)


def _s5_prep_kernel(lrc_ref, lic_ref, ldtc_ref, lrr_ref, lir_ref, ldtr_ref, br_ref, bi_ref, cr_ref, ci_ref,
                    w1r_ref, w1i_ref, w3xr_ref, w3xi_ref, w3u_ref, tab_ref):
    lr, li = lrc_ref[...], lic_ref[...]
    dt = jnp.exp(ldtc_ref[...])
    ar, th = lr * dt, li * dt

    def powers(kvec):
        mag = jnp.exp(ar * kvec)
        ang = th * kvec
        return mag * jnp.cos(ang), mag * jnp.sin(ang)

    lane = lax.broadcasted_iota(jnp.int32, (1, MXU_TILE), 1)
    step = lax.shift_right_logical(lane, 5)
    lane_group = lax.shift_right_logical(lane, 4) & 1
    mode = lax.broadcasted_iota(jnp.int32, (S5_MODES, 1), 0)
    own = lane_group == (lax.shift_right_logical(mode, 6) & 1)
    chan = lax.broadcasted_iota(jnp.int32, (S5_GROUP, 1), 0)
    rep = jnp.where((lane & (S5_GROUP - 1)) == chan, 1.0, 0.0)
    b_r, b_i = _split_dot(br_ref[...], rep), _split_dot(bi_ref[...], rep)
    c_r, c_i = _split_dot(cr_ref[...], rep), _split_dot(ci_ref[...], rep)

    ab_re, ab_im = powers(jnp.ones((1, 1), F32))
    nr, ni = ab_re - 1.0, ab_im
    den = lr * lr + li * li
    f_re = (nr * lr + ni * li) / den
    f_im = (ni * lr - nr * li) / den
    bbr, bbi = _cmul(f_re, f_im, b_r, b_i)

    p7r, p7i = powers((S5_STEP - 1 - step).astype(F32))
    w1r, w1i = _cmul(p7r, p7i, bbr, bbi)
    w1r_ref[...] = jnp.where(own, w1r, 0.0).astype(BF16)
    w1i_ref[...] = jnp.where(own, w1i, 0.0).astype(BF16)
    p1r, p1i = powers((step + 1).astype(F32))
    xr, xi = _cmul(p1r, p1i, c_r, c_i)
    w3xr_ref[...] = jnp.where(own, xr, 0.0).astype(BF16)
    w3xi_ref[...] = jnp.where(own, -xi, 0.0).astype(BF16)

    p0r, p0i = powers(step.astype(F32))
    car, cai = _cmul(p0r, p0i, c_r, c_i)
    car, cai = jnp.where(own, car, 0.0), jnp.where(own, cai, 0.0)
    lane_w = lax.broadcasted_iota(jnp.int32, (1, S5_WIDTH), 1)
    rep_w = jnp.where((lane_w & (S5_GROUP - 1)) == chan, 1.0, 0.0)
    own_w = lax.shift_right_logical(lane_w, 4) == lax.shift_right_logical(mode, 6)
    dbr, dbi = _cmul(f_re, f_im, _split_dot(br_ref[...], rep_w), _split_dot(bi_ref[...], rep_w))
    dbr, dbi = jnp.where(own_w, dbr, 0.0), jnp.where(own_w, dbi, 0.0)
    taps = _split_dot(dbr.T, car) - _split_dot(dbi.T, cai)
    for i in range(S5_STEP):
        blk = _shift_lanes_256(taps, PAIR_CH * i)
        w3u_ref[:, PAIR_CH * i:PAIR_CH * (i + 1), :] = blk.reshape(S5_PAIRS, PAIR_CH, MXU_TILE).astype(BF16)

    arr = lrr_ref[...] * jnp.exp(ldtr_ref[...])
    thr = lir_ref[...] * jnp.exp(ldtr_ref[...])
    pos = lax.broadcasted_iota(jnp.int32, (SUBLANES, 1), 0)

    def powers_rows(kcol):
        mag = jnp.exp(arr * kcol)
        ang = thr * kcol
        return mag * jnp.cos(ang), mag * jnp.sin(ang)

    for s, kstep in enumerate((1, 2, 4)):
        kr, ki = powers_rows(jnp.full((1, 1), float(S5_STEP * kstep), F32))
        tab_ref[2 * s] = jnp.where(pos >= kstep, kr, 0.0)
        tab_ref[2 * s + 1] = jnp.where(pos >= kstep, ki, 0.0)
    tab_ref[6], tab_ref[7] = powers_rows(((pos + 1) * S5_STEP).astype(F32))


def _s5_prep(lam_re, lam_im, log_dt, b_re, b_im, c_re, c_im):
    m = S5_MODES
    ldt = jnp.broadcast_to(log_dt.astype(F32)[:, None], (S5_GROUPS, S5_STATE))
    col = lambda a: a.astype(F32).reshape(m, 1)
    rowv = lambda a: a.astype(F32).reshape(1, m)
    bmat = lambda a: a.astype(F32).reshape(m, S5_GROUP)
    cmat = lambda a: a.astype(F32).transpose(0, 2, 1).reshape(m, S5_GROUP)
    tile = jax.ShapeDtypeStruct((m, MXU_TILE), BF16)
    out_shape = [tile] * 4 + [jax.ShapeDtypeStruct((S5_PAIRS, MXU_TILE, MXU_TILE), BF16),
                              jax.ShapeDtypeStruct((N_SCAN_TABLES, SUBLANES, m), F32)]
    w1r, w1i, w3xr, w3xi, w3u, tab = pl.pallas_call(
        _s5_prep_kernel,
        out_shape=out_shape,
        compiler_params=pltpu.CompilerParams(vmem_limit_bytes=VMEM_LIMIT),
        name="s5_prep",
    )(col(lam_re), col(lam_im), col(ldt), rowv(lam_re), rowv(lam_im), rowv(ldt),
      bmat(b_re), bmat(b_im), cmat(c_re), cmat(c_im))
    per_pair = lambda a: a.reshape(S5_PAIRS, PAIR_MODES, MXU_TILE)
    return per_pair(w1r), per_pair(w1i), per_pair(w3xr), per_pair(w3xi), w3u, tab


def _lane_window(lo):
    lane = lax.broadcasted_iota(jnp.int32, (1, LANES), 1)
    return (lane >= lo) & (lane < lo + PAIR_CH)


def _merge_windows(pieces):
    acc = pieces[0]
    for k in range(1, len(pieces)):
        acc = jnp.where(_lane_window(PAIR_CH * k), pieces[k], acc)
    return acc


def _rolled(cache, key, make, shift):
    if (key, shift) not in cache:
        x = make()
        cache[(key, shift)] = x if shift == 0 else pltpu.roll(x, shift, axis=1)
    return cache[(key, shift)]


def _s5_regroup_in(step_slab):
    per_tile = LANES // PAIR_CH
    cache, outs = {}, []
    for q in range(S5_PAIRS):
        halves = []
        for m in range(MXU_TILE // LANES):
            pieces = []
            for i in range(per_tile * m, per_tile * (m + 1)):
                c = q // per_tile
                pieces.append(_rolled(cache, (i, c), lambda i=i, c=c: step_slab(i, c),
                                      PAIR_CH * ((i - q) % per_tile)))
            halves.append(_merge_windows(pieces))
        outs.append(jnp.concatenate(halves, axis=1))
    return outs


def _s5_regroup_out(ys):
    per_tile = LANES // PAIR_CH
    cache, out = {}, []
    for j in range(S5_STEP):
        slabs = []
        for c in range(S5_SLABS):
            pieces = []
            for q in range(per_tile * c, per_tile * (c + 1)):
                m = j // per_tile
                pieces.append(_rolled(cache, (q, m), lambda q=q, m=m: ys[q][:, LANES * m:LANES * (m + 1)],
                                      PAIR_CH * ((q - j) % per_tile)))
            slabs.append(_merge_windows(pieces))
        out.append(slabs)
    return out


def _mix_body(q_ref, k_ref, la_ref, v_ref, r_ref, u5_ref, st0_ref, s50_ref, gh_ref,
              w1r_ref, w1i_ref, w3xr_ref, w3xi_ref, w3u_ref, d_ref, tab_ref,
              og_ref, g5_ref, sto_ref, s5o_ref,
              st_sc, car_sc, o_sc, g5_sc, *, tb, streams, first):
    nm = S5_MODES

    ti = lax.broadcasted_iota(jnp.int32, (CHUNK, CHUNK), 0)
    si = lax.broadcasted_iota(jnp.int32, (CHUNK, CHUNK), 1)
    causal = si <= ti
    tril = jnp.where(causal, 1.0, 0.0).astype(BF16)
    lane = lax.broadcasted_iota(jnp.int32, (1, GLA_KEY), 1) % LANES
    head_mask = (lane < GLA_DK, lane >= GLA_DK)
    nsub = CHUNK // SUB

    def chunk(c):
        rows = slice(c * CHUNK, (c + 1) * CHUNK)
        la = la_ref[rows, :]
        hi = la.astype(BF16)
        lo = (la - hi.astype(F32)).astype(BF16)
        b = _dot(tril, hi) + _dot(tril, lo)
        refs = [jnp.zeros((1, GLA_KEY), F32)] + [b[SUB * i - 1:SUB * i, :] for i in range(1, nsub)]
        rb = jnp.concatenate([jnp.broadcast_to(x, (SUB, GLA_KEY)) for x in refs], axis=0)
        q, k = q_ref[rows, :], k_ref[rows, :]
        b_last = b[CHUNK - 1:CHUNK, :]
        qt = q * jnp.exp(b - rb)
        qd = q * jnp.exp(b)
        qt = [jnp.where(m, qt, 0.0).astype(BF16) for m in head_mask]
        qd = [jnp.where(m, qd, 0.0).astype(BF16) for m in head_mask]
        kd = (k * jnp.exp(b_last - b)).astype(BF16)
        e_last = jnp.exp(b_last)
        kts = [(k * jnp.exp(jnp.minimum(x - b, EXP_CLAMP))).astype(BF16) for x in refs]
        for h in range(GLA_HEADS):
            pc = slice(LANES * (h // 2), LANES * (h // 2 + 1))
            qt_h = qt[h % 2][:, pc]
            sc = jnp.concatenate(
                [_dot_nt(qt_h[SUB * i:SUB * (i + 1)], kts[i][:, pc]) for i in range(nsub)], axis=0)
            pm = jnp.where(causal, sc, 0.0).astype(BF16)
            vh = v_ref[rows, GLA_DV * h:GLA_DV * (h + 1)]
            if streams:
                st = st0_ref[0, c * GLA_HEADS + h]
            else:
                st = jnp.where(first, st0_ref[0, h], st_sc[h]) if c == 0 else st_sc[h]
            o = _dot(pm, vh) + _dot_nt(qd[h % 2][:, pc], st.astype(BF16))
            o_sc[rows, GLA_DV * h:GLA_DV * (h + 1)] = o
            vt = vh.astype(F32).T.astype(BF16)
            st_new = st * e_last[:, pc] + _dot(vt, kd[:, pc])
            if streams:
                sto_ref[0, c * GLA_HEADS + h] = st_new
            else:
                st_sc[h] = st_new

    for c in range(tb // CHUNK):
        chunk(c)

    o_all = o_sc[...]
    normed = []
    for h in range(GLA_HEADS):
        oh = o_all[:, GLA_DV * h:GLA_DV * (h + 1)]
        normed.append(oh * lax.rsqrt(jnp.mean(oh * oh, axis=-1, keepdims=True) + EPS))
    r = r_ref[...]
    og = jnp.concatenate(normed, axis=1) * gh_ref[...] * (r * _sigmoid(r))
    og_ref[...] = og.astype(BF16)
    if not streams:
        sto_ref[0] = st_sc[...]

    tg = tb // S5_STEP
    ng = tg // SUBLANES
    step_slab = lambda i, c: u5_ref[c, pl.ds(i, tg, stride=S5_STEP), :]
    uq = [x.astype(BF16) for x in _s5_regroup_in(step_slab)]
    w = [_dot_nt(uq[q], jnp.concatenate([w1r_ref[q], w1i_ref[q]], axis=0)) for q in range(S5_PAIRS)]
    xr = jnp.concatenate([x[:, :LANES] for x in w], axis=1).reshape(ng, SUBLANES, nm)
    xi = jnp.concatenate([x[:, LANES:] for x in w], axis=1).reshape(ng, SUBLANES, nm)
    for s, kstep in enumerate((1, 2, 4)):
        ar, ai = tab_ref[2 * s], tab_ref[2 * s + 1]
        sr = pltpu.roll(xr, kstep, axis=1)
        sim = pltpu.roll(xi, kstep, axis=1)
        xr, xi = xr + ar * sr - ai * sim, xi + ar * sim + ai * sr
    pr, pi = tab_ref[6], tab_ref[7]
    first_row = lax.broadcasted_iota(jnp.int32, (SUBLANES, nm), 0) == 0
    if not streams:
        cr = jnp.where(first, jnp.broadcast_to(s50_ref[0, :, :nm], (SUBLANES, nm)), car_sc[:, :nm])
        ci = jnp.where(first, jnp.broadcast_to(s50_ref[0, :, nm:], (SUBLANES, nm)), car_sc[:, nm:])
    prev_r, prev_i = [], []
    for g in range(ng):
        if streams:
            cr = jnp.broadcast_to(s50_ref[0, g:g + 1, :nm], (SUBLANES, nm))
            ci = jnp.broadcast_to(s50_ref[0, g:g + 1, nm:], (SUBLANES, nm))
        yr = xr[g] + pr * cr - pi * ci
        yi = xi[g] + pr * ci + pi * cr
        prev_r.append(jnp.where(first_row, cr, pltpu.roll(yr, 1, axis=0)))
        prev_i.append(jnp.where(first_row, ci, pltpu.roll(yi, 1, axis=0)))
        if streams:
            s5o_ref[0, g:g + 1, :nm] = yr[SUBLANES - 1:SUBLANES]
            s5o_ref[0, g:g + 1, nm:] = yi[SUBLANES - 1:SUBLANES]
        else:
            cr = jnp.broadcast_to(yr[SUBLANES - 1:SUBLANES], (SUBLANES, nm))
            ci = jnp.broadcast_to(yi[SUBLANES - 1:SUBLANES], (SUBLANES, nm))
    if not streams:
        car_sc[:, :nm] = cr
        car_sc[:, nm:] = ci
        s5o_ref[0] = car_sc[0:1, :]
    prev_r = jnp.concatenate(prev_r, axis=0).astype(BF16)
    prev_i = jnp.concatenate(prev_i, axis=0).astype(BF16)
    ys = []
    for q in range(S5_PAIRS):
        ms = slice(PAIR_MODES * q, PAIR_MODES * (q + 1))
        lhs = jnp.concatenate([prev_r[:, ms], prev_i[:, ms], uq[q]], axis=1)
        ys.append(_dot(lhs, jnp.concatenate([w3xr_ref[q], w3xi_ref[q], w3u_ref[q]], axis=0)))
    conv = _s5_regroup_out(ys)
    for j in range(S5_STEP):
        for c in range(S5_SLABS):
            y = conv[j][c] + d_ref[:, LANES * c:LANES * (c + 1)] * step_slab(j, c)
            inner = math.sqrt(2.0 / math.pi) * (y + 0.044715 * (y * y * y))
            g5_sc[c, pl.ds(j, tg, stride=S5_STEP), :] = 0.5 * y * (1.0 + jnp.tanh(inner))
    g5_ref[...] = jnp.concatenate([g5_sc[c] for c in range(S5_SLABS)], axis=1).astype(BF16)


def _fm_kernel(x_ref, g1_ref, wg_ref, wu_ref, wd_ref, gmix_ref, win_ref, wga_ref, wgu_ref, bg_ref,
               st0_ref, s50_ref, gh_ref, w1r_ref, w1i_ref, w3xr_ref, w3xi_ref, w3u_ref, d_ref, tab_ref,
               h_ref, gmg_ref, gms_ref, og_ref, g5_ref, sto_ref, s5o_ref,
               q_sc, k_sc, la_sc, v_sc, r_sc, u5_sc, st_sc, car_sc, o_sc, g5_sc, *, tile, sub, seq_tiles, streams):
    s = pl.program_id(0)

    @pl.when(s == 0)
    def _():
        for ref in (q_sc, k_sc, la_sc, v_sc, r_sc, u5_sc, st_sc, car_sc):
            ref[...] = jnp.zeros(ref.shape, ref.dtype)

    first = lax.rem(jnp.maximum(s - 1, 0), seq_tiles) == 0
    _mix_body(q_sc, k_sc, la_sc, v_sc, r_sc, u5_sc, st0_ref, s50_ref, gh_ref,
              w1r_ref, w1i_ref, w3xr_ref, w3xi_ref, w3u_ref, d_ref, tab_ref,
              og_ref, g5_ref, sto_ref, s5o_ref, st_sc, car_sc, o_sc, g5_sc,
              tb=tile, streams=streams, first=first)
    for r0 in range(0, tile, sub):
        rows = slice(r0, r0 + sub)
        f = _front_rows(x_ref[rows, :], g1_ref, wg_ref, wu_ref, wd_ref, gmix_ref, win_ref, wga_ref, wgu_ref, bg_ref)
        h_ref[rows, :] = f['h']
        gmg_ref[rows, :] = f['gmg'].astype(BF16)
        gms_ref[rows, :] = f['gms'].astype(BF16)
        q_sc[rows, :] = f['q']
        k_sc[rows, :] = f['k']
        la_sc[rows, :] = f['la']
        v_sc[rows, :] = f['v'].astype(BF16)
        r_sc[rows, :] = f['r']
        for c in range(S5_SLABS):
            u5_sc[c, rows, :] = f['u5'][:, LANES * c:LANES * (c + 1)]


def _front_mix(x2d, st0, s50, p, tile, sub, seq_tiles, streams):
    n = x2d.shape[0]
    ntiles = n // tile
    cur = lambda s: jnp.minimum(s, ntiles - 1)
    prev = lambda s: jnp.maximum(s - 1, 0)
    row_cur = lambda w: pl.BlockSpec((tile, w), lambda s: (cur(s), 0))
    row_prev = lambda w: pl.BlockSpec((tile, w), lambda s: (prev(s), 0))
    st_spec = pl.BlockSpec((1,) + st0.shape[1:], lambda s: (prev(s) // seq_tiles, 0, 0, 0))
    s5_spec = pl.BlockSpec((1,) + s50.shape[1:], lambda s: (prev(s) // seq_tiles, 0, 0))
    pair_tile = _const_spec((S5_PAIRS, PAIR_MODES, MXU_TILE))
    in_specs = [row_cur(D_MODEL), _const_spec((1, D_MODEL)),
                _const_spec((D_MODEL, D_FF)), _const_spec((D_MODEL, D_FF)), _const_spec((D_FF, D_MODEL)),
                _const_spec((1, D_MODEL)), _const_spec((D_MODEL, W_MAIN_COLS)),
                _const_spec((D_MODEL, LANES)), _const_spec((LANES, GLA_KEY)), _const_spec((1, GLA_KEY)),
                st_spec, s5_spec, _const_spec((1, GLA_VAL)),
                pair_tile, pair_tile, pair_tile, pair_tile, _const_spec((S5_PAIRS, MXU_TILE, MXU_TILE)),
                _const_spec((1, S5_WIDTH)), _const_spec((N_SCAN_TABLES, SUBLANES, S5_MODES))]
    sds = lambda w, dt: jax.ShapeDtypeStruct((n, w), dt)
    out_shape = [sds(D_MODEL, F32), sds(D_MODEL, BF16), sds(D_MODEL, BF16), sds(GLA_VAL, BF16), sds(S5_WIDTH, BF16),
                 jax.ShapeDtypeStruct(st0.shape, F32), jax.ShapeDtypeStruct(s50.shape, F32)]
    out_specs = [row_cur(D_MODEL), row_cur(D_MODEL), row_cur(D_MODEL), row_prev(GLA_VAL), row_prev(S5_WIDTH),
                 st_spec, s5_spec]
    scratch = [pltpu.VMEM((tile, GLA_KEY), F32), pltpu.VMEM((tile, GLA_KEY), F32), pltpu.VMEM((tile, GLA_KEY), F32),
               pltpu.VMEM((tile, GLA_VAL), BF16), pltpu.VMEM((tile, GLA_VAL), F32),
               pltpu.VMEM((S5_SLABS, tile, LANES), F32),
               pltpu.VMEM((GLA_HEADS, GLA_DV, LANES), F32), pltpu.VMEM((SUBLANES, 2 * S5_MODES), F32),
               pltpu.VMEM((tile, GLA_VAL), F32), pltpu.VMEM((S5_SLABS, tile, LANES), F32)]
    return pl.pallas_call(
        functools.partial(_fm_kernel, tile=tile, sub=sub, seq_tiles=seq_tiles, streams=streams),
        out_shape=out_shape,
        grid=(ntiles + 1,),
        in_specs=in_specs,
        out_specs=out_specs,
        scratch_shapes=scratch,
        compiler_params=pltpu.CompilerParams(dimension_semantics=("arbitrary",), vmem_limit_bytes=VMEM_LIMIT),
        name="front_mix",
    )(x2d, p['g_ffn1'], p['wg1'], p['wu1'], p['wd1'], p['g_mix'], p['w_main'], p['w_ga'], p['w_gu'], p['b_gate'],
      st0, s50, p['g_gla_head'], *p['s5_tiles'], p['s5_d'], p['tab'])


def _back_kernel(h_ref, og_ref, g5_ref, gmg_ref, gms_ref, wgo_ref, wa_ref, wb_ref, wso_ref, wo_ref,
                 g2_ref, wg_ref, wu_ref, wd_ref, gf_ref, y_ref, *, sub):
    for r0 in range(0, h_ref.shape[0], sub):
        rows = slice(r0, r0 + sub)
        y_gla = _dot(og_ref[rows, :], wgo_ref[...])
        g5 = g5_ref[rows, :]
        glu = _dot(g5, wa_ref[...]) * _sigmoid(_dot(g5, wb_ref[...]))
        y_s5 = _dot(glu.astype(BF16), wso_ref[...])
        m = _sigmoid(gmg_ref[rows, :].astype(F32)) * y_gla + _sigmoid(gms_ref[rows, :].astype(F32)) * y_s5
        h = h_ref[rows, :] + _dot(m.astype(BF16), wo_ref[...])
        hn = _rms(h, g2_ref[...]).astype(BF16)
        h = h + 0.5 * _swiglu(hn, wg_ref, wu_ref, wd_ref)
        y_ref[rows, :] = _rms(h, gf_ref[...])


def _back(h, og, g5, gmg, gms, p, tm, sub):
    n = h.shape[0]
    row = lambda w: pl.BlockSpec((tm, w), lambda i: (i, 0))
    in_specs = [row(D_MODEL), row(GLA_VAL), row(S5_WIDTH), row(D_MODEL), row(D_MODEL),
                _const_spec((GLA_VAL, D_MODEL)), _const_spec((S5_WIDTH, S5_WIDTH)), _const_spec((S5_WIDTH, S5_WIDTH)),
                _const_spec((S5_WIDTH, D_MODEL)), _const_spec((D_MODEL, D_MODEL)), _const_spec((1, D_MODEL)),
                _const_spec((D_MODEL, D_FF)), _const_spec((D_MODEL, D_FF)), _const_spec((D_FF, D_MODEL)),
                _const_spec((1, D_MODEL))]
    return pl.pallas_call(
        functools.partial(_back_kernel, sub=sub),
        out_shape=jax.ShapeDtypeStruct((n, D_MODEL), F32),
        grid=(n // tm,),
        in_specs=in_specs,
        out_specs=row(D_MODEL),
        compiler_params=pltpu.CompilerParams(dimension_semantics=("parallel",), vmem_limit_bytes=VMEM_LIMIT),
        name="back",
    )(h, og, g5, gmg, gms, p['w_gla_out'], p['w_glu_a'], p['w_glu_b'], p['w_s5_out'], p['w_out'],
      p['g_ffn2'], p['wg2'], p['wu2'], p['wd2'], p['g_final'])


def _gla_state_in(s, nb):
    st = jnp.swapaxes(s.astype(F32), 2, 3)
    z = jnp.zeros_like(st)
    sel = (jnp.arange(GLA_HEADS) % 2 == 0)[None, :, None, None]
    st = jnp.where(sel, jnp.concatenate([st, z], axis=-1), jnp.concatenate([z, st], axis=-1))
    return st.reshape(nb, -1, GLA_DV, LANES)


def _gla_state_out(st, bsz):
    st = st.reshape(bsz, GLA_HEADS, GLA_DV, LANES)
    sel = (jnp.arange(GLA_HEADS) % 2 == 0)[None, :, None, None]
    return jnp.swapaxes(jnp.where(sel, st[..., :GLA_DK], st[..., GLA_DK:]), 2, 3)


def _s5_state_in(x0, nb):
    x = x0.astype(F32).reshape(x0.shape[0], S5_MODES, 2)
    return jnp.concatenate([x[..., 0], x[..., 1]], axis=-1).reshape(nb, -1, 2 * S5_MODES)


def _s5_state_out(x, bsz):
    x = x.reshape(bsz, 2, S5_MODES)
    return jnp.stack([x[:, 0], x[:, 1]], axis=-1).reshape(bsz, S5_GROUPS, S5_STATE, 2)


def _layer(x, s_gla, s_s5, p, tile, sub, streams):
    bsz, t, _ = x.shape
    n = bsz * t
    seq_tiles = 1 if streams else t // tile
    nblk = n // (tile * seq_tiles)
    h, gmg, gms, og, g5, st, s5 = _front_mix(x.reshape(n, D_MODEL), _gla_state_in(s_gla, nblk),
                                             _s5_state_in(s_s5, nblk), p, tile, sub, seq_tiles, streams)
    y = _back(h, og, g5, gmg, gms, p, tile, sub)
    return y.reshape(bsz, t, D_MODEL), _gla_state_out(st, bsz), _s5_state_out(s5, bsz)


def kernel(x_prompt, x_sample, state_gla, state_s5, g_ffn1, w_ffn1_gate, w_ffn1_up, w_ffn1_down, g_mix, w_in, w_gate_up, b_gate, g_gla_head, w_gla_out, s5_lam_re, s5_lam_im, s5_log_dt, s5_b_re, s5_b_im, s5_c_re, s5_c_im, s5_d, w_glu_a, w_glu_b, w_s5_out, w_out, g_ffn2, w_ffn2_gate, w_ffn2_up, w_ffn2_down, g_final):
    vec = lambda a: a.astype(F32).reshape(1, -1)
    sizes = (GLA_KEY, GLA_KEY, GLA_VAL, GLA_VAL, GLA_GATE_RANK, S5_WIDTH, D_MODEL, D_MODEL)
    offs = [sum(sizes[:i]) for i in range(len(sizes) + 1)]
    wq, wk, wv, wr, wga, wu5, wgg, wgs = [w_in[:, offs[i]:offs[i + 1]] for i in range(len(sizes))]
    *s5_tiles, tab = _s5_prep(s5_lam_re, s5_lam_im, s5_log_dt, s5_b_re, s5_b_im, s5_c_re, s5_c_im)
    p = {
        'g_ffn1': vec(g_ffn1), 'wg1': w_ffn1_gate.astype(BF16), 'wu1': w_ffn1_up.astype(BF16),
        'wd1': w_ffn1_down.astype(BF16), 'g_mix': vec(g_mix),
        'w_main': jnp.concatenate([wq, wk, wv, wr, wu5, wgg, wgs], axis=1).astype(BF16),
        'w_ga': jnp.pad(wga, ((0, 0), (0, LANES - GLA_GATE_RANK))).astype(BF16),
        'w_gu': jnp.pad(w_gate_up, ((0, LANES - GLA_GATE_RANK), (0, 0))).astype(BF16),
        'b_gate': vec(b_gate), 'g_gla_head': vec(g_gla_head),
        's5_tiles': s5_tiles, 'tab': tab, 's5_d': vec(s5_d),
        'w_gla_out': w_gla_out.astype(BF16), 'w_glu_a': w_glu_a.astype(BF16), 'w_glu_b': w_glu_b.astype(BF16),
        'w_s5_out': w_s5_out.astype(BF16), 'w_out': w_out.astype(BF16),
        'g_ffn2': vec(g_ffn2), 'wg2': w_ffn2_gate.astype(BF16), 'wu2': w_ffn2_up.astype(BF16),
        'wd2': w_ffn2_down.astype(BF16), 'g_final': vec(g_final),
    }
    bp = x_prompt.shape[0]
    zero_gla = jnp.zeros((bp, GLA_HEADS, GLA_DK, GLA_DV), F32)
    zero_s5 = jnp.zeros((bp, S5_GROUPS, S5_STATE, 2), F32)
    y_p, gla_p, s5_p = _layer(x_prompt, zero_gla, zero_s5, p, tile=512, sub=256, streams=False)
    y_s, gla_s, s5_s = _layer(x_sample, state_gla, state_s5, p, tile=256, sub=128, streams=True)
    return (y_p, y_s, gla_p, s5_p, gla_s, s5_s)
```

```python
import functools
import math

import jax
import jax.numpy as jnp
from jax import lax
from jax.experimental import pallas as pl
from jax.experimental.pallas import tpu as pltpu

F32 = jnp.float32
BF16 = jnp.bfloat16

D_MODEL = 1024
D_FF = 2816
CHUNK = 64
SUB = 16
GLA_HEADS = 4
GLA_DK = 64
GLA_DV = 128
GLA_KEY = GLA_HEADS * GLA_DK
GLA_VAL = GLA_HEADS * GLA_DV
GLA_GATE_RANK = 16
GLA_GATE_TAU = 16.0
S5_WIDTH = 512
S5_GROUP = 16
S5_GROUPS = 32
S5_STATE = 64
S5_MODES = S5_GROUPS * S5_STATE
EPS = 1e-6
SUBLANES = 8
LANES = 128
MXU_TILE = 256
S5_STEP = SUBLANES
S5_PAIRS = S5_GROUPS // 2
PAIR_CH = 2 * S5_GROUP
PAIR_MODES = 2 * S5_STATE
S5_SLABS = S5_WIDTH // LANES
N_SCAN_TABLES = 8
EXP_CLAMP = 80.0
FF_CHUNK = 512
VMEM_LIMIT = 60 * 1024 * 1024

_OQ, _OK, _OV, _OR, _OU, _OGG, _OGS = 0, 256, 512, 1024, 1536, 2048, 3072
W_MAIN_COLS = 4096


def _dot(a, b):
    return jnp.dot(a, b, preferred_element_type=F32)


def _dot_nt(a, b):
    return lax.dot_general(a, b, (((1,), (1,)), ((), ())), preferred_element_type=F32)


def _split_dot(a, b):
    a1 = a.astype(BF16)
    a2 = (a - a1.astype(F32)).astype(BF16)
    b1 = b.astype(BF16)
    b2 = (b - b1.astype(F32)).astype(BF16)
    return _dot(a1, b1) + _dot(a1, b2) + _dot(a2, b1)


def _cmul(ar, ai, br, bi):
    return ar * br - ai * bi, ar * bi + ai * br


def _sigmoid(x):
    return 1.0 / (1.0 + jnp.exp(-x))


def _rms(x, g):
    ms = jnp.mean(x * x, axis=-1, keepdims=True)
    return x * lax.rsqrt(ms + EPS) * g


def _ff_chunks():
    out, c0 = [], 0
    while c0 < D_FF:
        n = min(FF_CHUNK, D_FF - c0)
        out.append((c0, n))
        c0 += n
    return out


def _swiglu(xn, wg_ref, wu_ref, wd_ref):
    acc = None
    for c0, n in _ff_chunks():
        g = _dot(xn, wg_ref[:, c0:c0 + n])
        u = _dot(xn, wu_ref[:, c0:c0 + n])
        a = (g * _sigmoid(g) * u).astype(BF16)
        part = _dot(a, wd_ref[c0:c0 + n, :])
        acc = part if acc is None else acc + part
    return acc


def _const_spec(shape):
    nd = len(shape)
    return pl.BlockSpec(shape, lambda *_: (0,) * nd, pipeline_mode=pl.Buffered(1))


def _front_stages(x_ref, rows, g1_ref, wg_ref, wu_ref, wd_ref, gmix_ref, win_ref, wga_ref, wgu_ref, bg_ref, sink):
    x = x_ref[rows, :]
    xn = _rms(x, g1_ref[...]).astype(BF16)
    chunks = _ff_chunks()
    per_piece = -(-len(chunks) // 3)
    acc = None
    for i, (c0, n) in enumerate(chunks):
        g = _dot(xn, wg_ref[:, c0:c0 + n])
        u = _dot(xn, wu_ref[:, c0:c0 + n])
        a = (g * _sigmoid(g) * u).astype(BF16)
        part = _dot(a, wd_ref[c0:c0 + n, :])
        acc = part if acc is None else acc + part
        if i % per_piece == per_piece - 1 and i + 1 < len(chunks):
            yield
    h = x + 0.5 * acc
    un = _rms(h, gmix_ref[...]).astype(BF16)
    yield

    def proj(c0, n):
        return _dot(un, win_ref[:, c0:c0 + n])

    ga = _dot(un, wga_ref[...])
    pre = _dot(ga.astype(BF16), wgu_ref[...]) + bg_ref[...]
    log_sig = jnp.minimum(pre, 0.0) - jnp.log(1.0 + jnp.exp(-jnp.abs(pre)))
    sink(dict(h=h, q=proj(_OQ, GLA_KEY) * (GLA_DK ** -0.5), k=proj(_OK, GLA_KEY), v=proj(_OV, GLA_VAL),
              r=proj(_OR, GLA_VAL), u5=proj(_OU, S5_WIDTH), gmg=proj(_OGG, D_MODEL), gms=proj(_OGS, D_MODEL),
              la=log_sig * (1.0 / GLA_GATE_TAU)))


def _shift_lanes_256(x, s):
    lane = lax.broadcasted_iota(jnp.int32, (1, LANES), 1)
    lo, hi = x[:, :LANES], x[:, LANES:]
    if s == 0:
        return x
    if s >= LANES:
        r = s - LANES
        moved = lo if r == 0 else jnp.where(lane >= r, pltpu.roll(lo, r, axis=1), 0.0)
        return jnp.concatenate([jnp.zeros_like(lo), moved], axis=1)
    rl, rh = pltpu.roll(lo, s, axis=1), pltpu.roll(hi, s, axis=1)
    return jnp.concatenate([jnp.where(lane >= s, rl, 0.0), jnp.where(lane >= s, rh, rl)], axis=1)


def _s5_prep_kernel(lrc_ref, lic_ref, ldtc_ref, lrr_ref, lir_ref, ldtr_ref, br_ref, bi_ref, cr_ref, ci_ref,
                    w1r_ref, w1i_ref, w3xr_ref, w3xi_ref, w3u_ref, tab_ref):
    lr, li = lrc_ref[...], lic_ref[...]
    dt = jnp.exp(ldtc_ref[...])
    ar, th = lr * dt, li * dt

    def powers(kvec):
        mag = jnp.exp(ar * kvec)
        ang = th * kvec
        return mag * jnp.cos(ang), mag * jnp.sin(ang)

    lane = lax.broadcasted_iota(jnp.int32, (1, MXU_TILE), 1)
    step = lax.shift_right_logical(lane, 5)
    lane_group = lax.shift_right_logical(lane, 4) & 1
    mode = lax.broadcasted_iota(jnp.int32, (S5_MODES, 1), 0)
    own = lane_group == (lax.shift_right_logical(mode, 6) & 1)
    chan = lax.broadcasted_iota(jnp.int32, (S5_GROUP, 1), 0)
    rep = jnp.where((lane & (S5_GROUP - 1)) == chan, 1.0, 0.0)
    b_r, b_i = _split_dot(br_ref[...], rep), _split_dot(bi_ref[...], rep)
    c_r, c_i = _split_dot(cr_ref[...], rep), _split_dot(ci_ref[...], rep)

    ab_re, ab_im = powers(jnp.ones((1, 1), F32))
    nr, ni = ab_re - 1.0, ab_im
    den = lr * lr + li * li
    f_re = (nr * lr + ni * li) / den
    f_im = (ni * lr - nr * li) / den
    bbr, bbi = _cmul(f_re, f_im, b_r, b_i)

    p7r, p7i = powers((S5_STEP - 1 - step).astype(F32))
    w1r, w1i = _cmul(p7r, p7i, bbr, bbi)
    w1r_ref[...] = jnp.where(own, w1r, 0.0).astype(BF16)
    w1i_ref[...] = jnp.where(own, w1i, 0.0).astype(BF16)
    p1r, p1i = powers((step + 1).astype(F32))
    xr, xi = _cmul(p1r, p1i, c_r, c_i)
    w3xr_ref[...] = jnp.where(own, xr, 0.0).astype(BF16)
    w3xi_ref[...] = jnp.where(own, -xi, 0.0).astype(BF16)

    p0r, p0i = powers(step.astype(F32))
    car, cai = _cmul(p0r, p0i, c_r, c_i)
    car, cai = jnp.where(own, car, 0.0), jnp.where(own, cai, 0.0)
    lane_w = lax.broadcasted_iota(jnp.int32, (1, S5_WIDTH), 1)
    rep_w = jnp.where((lane_w & (S5_GROUP - 1)) == chan, 1.0, 0.0)
    own_w = lax.shift_right_logical(lane_w, 4) == lax.shift_right_logical(mode, 6)
    dbr, dbi = _cmul(f_re, f_im, _split_dot(br_ref[...], rep_w), _split_dot(bi_ref[...], rep_w))
    dbr, dbi = jnp.where(own_w, dbr, 0.0), jnp.where(own_w, dbi, 0.0)
    taps = _split_dot(dbr.T, car) - _split_dot(dbi.T, cai)
    for i in range(S5_STEP):
        blk = _shift_lanes_256(taps, PAIR_CH * i)
        w3u_ref[:, PAIR_CH * i:PAIR_CH * (i + 1), :] = blk.reshape(S5_PAIRS, PAIR_CH, MXU_TILE).astype(BF16)

    arr = lrr_ref[...] * jnp.exp(ldtr_ref[...])
    thr = lir_ref[...] * jnp.exp(ldtr_ref[...])
    pos = lax.broadcasted_iota(jnp.int32, (SUBLANES, 1), 0)

    def powers_rows(kcol):
        mag = jnp.exp(arr * kcol)
        ang = thr * kcol
        return mag * jnp.cos(ang), mag * jnp.sin(ang)

    for s, kstep in enumerate((1, 2, 4)):
        kr, ki = powers_rows(jnp.full((1, 1), float(S5_STEP * kstep), F32))
        tab_ref[2 * s] = jnp.where(pos >= kstep, kr, 0.0)
        tab_ref[2 * s + 1] = jnp.where(pos >= kstep, ki, 0.0)
    tab_ref[6], tab_ref[7] = powers_rows(((pos + 1) * S5_STEP).astype(F32))


def _s5_prep(lam_re, lam_im, log_dt, b_re, b_im, c_re, c_im):
    m = S5_MODES
    ldt = jnp.broadcast_to(log_dt.astype(F32)[:, None], (S5_GROUPS, S5_STATE))
    col = lambda a: a.astype(F32).reshape(m, 1)
    rowv = lambda a: a.astype(F32).reshape(1, m)
    bmat = lambda a: a.astype(F32).reshape(m, S5_GROUP)
    cmat = lambda a: a.astype(F32).transpose(0, 2, 1).reshape(m, S5_GROUP)
    tile = jax.ShapeDtypeStruct((m, MXU_TILE), BF16)
    out_shape = [tile] * 4 + [jax.ShapeDtypeStruct((S5_PAIRS, MXU_TILE, MXU_TILE), BF16),
                              jax.ShapeDtypeStruct((N_SCAN_TABLES, SUBLANES, m), F32)]
    w1r, w1i, w3xr, w3xi, w3u, tab = pl.pallas_call(
        _s5_prep_kernel,
        out_shape=out_shape,
        compiler_params=pltpu.CompilerParams(vmem_limit_bytes=VMEM_LIMIT),
        name="s5_prep",
    )(col(lam_re), col(lam_im), col(ldt), rowv(lam_re), rowv(lam_im), rowv(ldt),
      bmat(b_re), bmat(b_im), cmat(c_re), cmat(c_im))
    per_pair = lambda a: a.reshape(S5_PAIRS, PAIR_MODES, MXU_TILE)
    return per_pair(w1r), per_pair(w1i), per_pair(w3xr), per_pair(w3xi), w3u, tab


def _lane_window(lo):
    lane = lax.broadcasted_iota(jnp.int32, (1, LANES), 1)
    return (lane >= lo) & (lane < lo + PAIR_CH)


def _merge_windows(pieces):
    acc = pieces[0]
    for k in range(1, len(pieces)):
        acc = jnp.where(_lane_window(PAIR_CH * k), pieces[k], acc)
    return acc


def _rolled(cache, key, make, shift):
    if (key, shift) not in cache:
        x = make()
        cache[(key, shift)] = x if shift == 0 else pltpu.roll(x, shift, axis=1)
    return cache[(key, shift)]


def _s5_regroup_in(step_slab):
    per_tile = LANES // PAIR_CH
    cache, outs = {}, []
    for q in range(S5_PAIRS):
        halves = []
        for m in range(MXU_TILE // LANES):
            pieces = []
            for i in range(per_tile * m, per_tile * (m + 1)):
                c = q // per_tile
                pieces.append(_rolled(cache, (i, c), lambda i=i, c=c: step_slab(i, c),
                                      PAIR_CH * ((i - q) % per_tile)))
            halves.append(_merge_windows(pieces))
        outs.append(jnp.concatenate(halves, axis=1))
    return outs


def _s5_regroup_out(ys):
    per_tile = LANES // PAIR_CH
    cache, out = {}, []
    for j in range(S5_STEP):
        slabs = []
        for c in range(S5_SLABS):
            pieces = []
            for q in range(per_tile * c, per_tile * (c + 1)):
                m = j // per_tile
                pieces.append(_rolled(cache, (q, m), lambda q=q, m=m: ys[q][:, LANES * m:LANES * (m + 1)],
                                      PAIR_CH * ((q - j) % per_tile)))
            slabs.append(_merge_windows(pieces))
        out.append(slabs)
    return out


def _mix_stages(q_ref, k_ref, la_ref, v_ref, r_ref, u5_ref, st0_ref, s50_ref, gh_ref,
                w1r_ref, w1i_ref, w3xr_ref, w3xi_ref, w3u_ref, d_ref, tab_ref,
                og_ref, g5_ref, sto_ref, s5o_ref,
                st_sc, car_sc, o_sc, g5_sc, *, tb, streams, first):
    nm = S5_MODES
    nch = tb // CHUNK
    crow = lambda c: slice(c * CHUNK, (c + 1) * CHUNK)
    ti = lax.broadcasted_iota(jnp.int32, (CHUNK, CHUNK), 0)
    si = lax.broadcasted_iota(jnp.int32, (CHUNK, CHUNK), 1)
    tril = jnp.where(si <= ti, 1.0, 0.0).astype(BF16)
    lane = lax.broadcasted_iota(jnp.int32, (1, GLA_KEY), 1) % LANES
    head_mask = (lane < GLA_DK, lane >= GLA_DK)
    nsub = CHUNK // SUB
    npairs = GLA_HEADS // 2
    pt = lax.broadcasted_iota(jnp.int32, (2 * CHUNK, LANES), 0) & (CHUNK - 1)
    ps = lax.broadcasted_iota(jnp.int32, (2 * CHUNK, LANES), 1)
    causal_pair = ps <= pt
    sr = lax.broadcasted_iota(jnp.int32, (2 * GLA_DV, LANES), 0)
    sl = lax.broadcasted_iota(jnp.int32, (2 * GLA_DV, LANES), 1)
    own_head = (sr < GLA_DV) == (sl < GLA_DK)

    def row_block(x, i):
        parts = [jnp.zeros((SUB, x.shape[1]), x.dtype)] * nsub
        parts[i] = x[SUB * i:SUB * (i + 1)]
        return jnp.concatenate(parts, axis=0)

    bs = []
    for c in range(nch):
        la = la_ref[crow(c), :]
        hi = la.astype(BF16)
        lo = (la - hi.astype(F32)).astype(BF16)
        bs.append(_dot(tril, hi) + _dot(tril, lo))
    yield

    pms, qds, upds, els = [], [], [], []
    for c in range(nch):
        b = bs[c]
        refs = [jnp.zeros((1, GLA_KEY), F32)] + [b[SUB * i - 1:SUB * i, :] for i in range(1, nsub)]
        rb = jnp.concatenate([jnp.broadcast_to(x, (SUB, GLA_KEY)) for x in refs], axis=0)
        q, k = q_ref[crow(c), :], k_ref[crow(c), :]
        b_last = b[CHUNK - 1:CHUNK, :]
        qt = q * jnp.exp(b - rb)
        qt = [jnp.where(m, qt, 0.0).astype(BF16) for m in head_mask]
        qds.append((q * jnp.exp(b)).astype(BF16))
        kd = (k * jnp.exp(b_last - b)).astype(BF16)
        els.append(jnp.exp(b_last))
        kts = [(k * jnp.exp(jnp.minimum(x - b, EXP_CLAMP))).astype(BF16) for x in refs]
        pm_c, upd_c = [], []
        for pp in range(npairs):
            pc = slice(LANES * pp, LANES * (pp + 1))
            lhs = jnp.concatenate(
                [jnp.concatenate([row_block(qt[e][:, pc], i) for i in range(nsub)], axis=1) for e in (0, 1)], axis=0)
            rhs = jnp.concatenate([kts[i][:, pc] for i in range(nsub)], axis=1)
            rhs = jnp.concatenate([rhs, jnp.zeros_like(rhs)], axis=0)
            pm_c.append(jnp.where(causal_pair, _dot_nt(lhs, rhs), 0.0).astype(BF16))
            vt = v_ref[crow(c), 2 * GLA_DV * pp:2 * GLA_DV * (pp + 1)].astype(F32).T.astype(BF16)
            upd_c.append(_dot(vt, kd[:, pc]))
        pms.append(pm_c)
        upds.append(upd_c)
    r = r_ref[...]
    silu_r = r * _sigmoid(r)
    tg = tb // S5_STEP
    ng = tg // SUBLANES
    step_slab = lambda i, c: u5_ref[c, pl.ds(i, tg, stride=S5_STEP), :]
    uq = [x.astype(BF16) for x in _s5_regroup_in(step_slab)]
    w = [_dot_nt(uq[q], jnp.concatenate([w1r_ref[q], w1i_ref[q]], axis=0)) for q in range(S5_PAIRS)]
    skip = [[d_ref[:, LANES * c:LANES * (c + 1)] * step_slab(j, c) for c in range(S5_SLABS)] for j in range(S5_STEP)]
    yield

    for pp in range(npairs):
        pc = slice(LANES * pp, LANES * (pp + 1))
        vc = slice(2 * GLA_DV * pp, 2 * GLA_DV * (pp + 1))
        if not streams:
            st = jnp.where(first, st0_ref[0, pp], st_sc[pp])
        for c in range(nch):
            if streams:
                st = st0_ref[0, c * npairs + pp]
            vp = v_ref[crow(c), vc]
            z = jnp.zeros((CHUNK, GLA_DV), BF16)
            zz = jnp.zeros((CHUNK, 2 * GLA_DV), BF16)
            v_diag = jnp.concatenate([jnp.concatenate([vp[:, :GLA_DV], z], axis=1), zz,
                                      jnp.concatenate([z, vp[:, GLA_DV:]], axis=1), zz], axis=0)
            pm = pms[c][pp]
            o_sc[crow(c), vc] = (_dot(jnp.concatenate([pm[:CHUNK], pm[CHUNK:]], axis=1), v_diag)
                                 + _dot_nt(qds[c][:, pc], st.astype(BF16)))
            st = jnp.where(own_head, st * els[c][:, pc] + upds[c][pp], 0.0)
            if streams:
                sto_ref[0, c * npairs + pp] = st
        if not streams:
            st_sc[pp] = st
    if not streams:
        sto_ref[0] = st_sc[...]
    yield

    o_all = o_sc[...]
    normed = []
    for h in range(GLA_HEADS):
        oh = o_all[:, GLA_DV * h:GLA_DV * (h + 1)]
        normed.append(oh * lax.rsqrt(jnp.mean(oh * oh, axis=-1, keepdims=True) + EPS))
    og_ref[...] = (jnp.concatenate(normed, axis=1) * gh_ref[...] * silu_r).astype(BF16)
    xr = jnp.concatenate([x[:, :LANES] for x in w], axis=1).reshape(ng, SUBLANES, nm)
    xi = jnp.concatenate([x[:, LANES:] for x in w], axis=1).reshape(ng, SUBLANES, nm)
    for s, kstep in enumerate((1, 2, 4)):
        ar, ai = tab_ref[2 * s], tab_ref[2 * s + 1]
        sre = pltpu.roll(xr, kstep, axis=1)
        sim = pltpu.roll(xi, kstep, axis=1)
        xr, xi = xr + ar * sre - ai * sim, xi + ar * sim + ai * sre
    pr, pi = tab_ref[6], tab_ref[7]
    first_row = lax.broadcasted_iota(jnp.int32, (SUBLANES, nm), 0) == 0
    if not streams:
        cr = jnp.where(first, jnp.broadcast_to(s50_ref[0, :, :nm], (SUBLANES, nm)), car_sc[:, :nm])
        ci = jnp.where(first, jnp.broadcast_to(s50_ref[0, :, nm:], (SUBLANES, nm)), car_sc[:, nm:])
    prev_r, prev_i = [], []
    for g in range(ng):
        if streams:
            cr = jnp.broadcast_to(s50_ref[0, g:g + 1, :nm], (SUBLANES, nm))
            ci = jnp.broadcast_to(s50_ref[0, g:g + 1, nm:], (SUBLANES, nm))
        yr = xr[g] + pr * cr - pi * ci
        yi = xi[g] + pr * ci + pi * cr
        prev_r.append(jnp.where(first_row, cr, pltpu.roll(yr, 1, axis=0)))
        prev_i.append(jnp.where(first_row, ci, pltpu.roll(yi, 1, axis=0)))
        if streams:
            s5o_ref[0, g:g + 1, :nm] = yr[SUBLANES - 1:SUBLANES]
            s5o_ref[0, g:g + 1, nm:] = yi[SUBLANES - 1:SUBLANES]
        else:
            cr = jnp.broadcast_to(yr[SUBLANES - 1:SUBLANES], (SUBLANES, nm))
            ci = jnp.broadcast_to(yi[SUBLANES - 1:SUBLANES], (SUBLANES, nm))
    if not streams:
        car_sc[:, :nm] = cr
        car_sc[:, nm:] = ci
        s5o_ref[0] = car_sc[0:1, :]
    prev_r = jnp.concatenate(prev_r, axis=0).astype(BF16)
    prev_i = jnp.concatenate(prev_i, axis=0).astype(BF16)
    ys = []
    for q in range(S5_PAIRS):
        ms = slice(PAIR_MODES * q, PAIR_MODES * (q + 1))
        lhs = jnp.concatenate([prev_r[:, ms], prev_i[:, ms], uq[q]], axis=1)
        ys.append(_dot(lhs, jnp.concatenate([w3xr_ref[q], w3xi_ref[q], w3u_ref[q]], axis=0)))
    yield

    conv = _s5_regroup_out(ys)
    for j in range(S5_STEP):
        for c in range(S5_SLABS):
            y = conv[j][c] + skip[j][c]
            inner = math.sqrt(2.0 / math.pi) * (y + 0.044715 * (y * y * y))
            g5_sc[c, pl.ds(j, tg, stride=S5_STEP), :] = 0.5 * y * (1.0 + jnp.tanh(inner))
    g5_ref[...] = jnp.concatenate([g5_sc[c] for c in range(S5_SLABS)], axis=1).astype(BF16)


def _fm_kernel(x_ref, g1_ref, wg_ref, wu_ref, wd_ref, gmix_ref, win_ref, wga_ref, wgu_ref, bg_ref,
               st0_ref, s50_ref, gh_ref, w1r_ref, w1i_ref, w3xr_ref, w3xi_ref, w3u_ref, d_ref, tab_ref,
               h_ref, gmg_ref, gms_ref, og_ref, g5_ref, sto_ref, s5o_ref,
               q_sc, k_sc, la_sc, v_sc, r_sc, u5_sc, st_sc, car_sc, o_sc, g5_sc, *, tile, sub, seq_tiles, streams):
    s = pl.program_id(0)

    @pl.when(s == 0)
    def _():
        for ref in (q_sc, k_sc, la_sc, v_sc, r_sc, u5_sc, st_sc, car_sc):
            ref[...] = jnp.zeros(ref.shape, ref.dtype)

    first = lax.rem(jnp.maximum(s - 1, 0), seq_tiles) == 0
    mix = _mix_stages(q_sc, k_sc, la_sc, v_sc, r_sc, u5_sc, st0_ref, s50_ref, gh_ref,
                      w1r_ref, w1i_ref, w3xr_ref, w3xi_ref, w3u_ref, d_ref, tab_ref,
                      og_ref, g5_ref, sto_ref, s5o_ref, st_sc, car_sc, o_sc, g5_sc,
                      tb=tile, streams=streams, first=first)

    def front(r0):
        rows = slice(r0, r0 + sub)

        def sink(f):
            h_ref[rows, :] = f['h']
            gmg_ref[rows, :] = f['gmg'].astype(BF16)
            gms_ref[rows, :] = f['gms'].astype(BF16)
            q_sc[rows, :] = f['q']
            k_sc[rows, :] = f['k']
            la_sc[rows, :] = f['la']
            v_sc[rows, :] = f['v'].astype(BF16)
            r_sc[rows, :] = f['r']
            for c in range(S5_SLABS):
                u5_sc[c, rows, :] = f['u5'][:, LANES * c:LANES * (c + 1)]

        return _front_stages(x_ref, rows, g1_ref, wg_ref, wu_ref, wd_ref, gmix_ref, win_ref, wga_ref, wgu_ref,
                             bg_ref, sink)

    fronts = [front(r0) for r0 in range(0, tile, sub)]
    fa, rest = fronts[0], fronts[1:]
    for g in (mix, fa, mix, fa, mix, fa, fa, mix):
        next(g, None)
    if rest:
        next(rest[0], None)
    next(mix, None)
    for g in rest:
        for _ in g:
            pass


def _front_mix(x2d, st0, s50, p, tile, sub, seq_tiles, streams):
    n = x2d.shape[0]
    ntiles = n // tile
    cur = lambda s: jnp.minimum(s, ntiles - 1)
    prev = lambda s: jnp.maximum(s - 1, 0)
    row_cur = lambda w: pl.BlockSpec((tile, w), lambda s: (cur(s), 0))
    row_prev = lambda w: pl.BlockSpec((tile, w), lambda s: (prev(s), 0))
    st_spec = pl.BlockSpec((1,) + st0.shape[1:], lambda s: (prev(s) // seq_tiles, 0, 0, 0))
    s5_spec = pl.BlockSpec((1,) + s50.shape[1:], lambda s: (prev(s) // seq_tiles, 0, 0))
    pair_tile = _const_spec((S5_PAIRS, PAIR_MODES, MXU_TILE))
    in_specs = [row_cur(D_MODEL), _const_spec((1, D_MODEL)),
                _const_spec((D_MODEL, D_FF)), _const_spec((D_MODEL, D_FF)), _const_spec((D_FF, D_MODEL)),
                _const_spec((1, D_MODEL)), _const_spec((D_MODEL, W_MAIN_COLS)),
                _const_spec((D_MODEL, LANES)), _const_spec((LANES, GLA_KEY)), _const_spec((1, GLA_KEY)),
                st_spec, s5_spec, _const_spec((1, GLA_VAL)),
                pair_tile, pair_tile, pair_tile, pair_tile, _const_spec((S5_PAIRS, MXU_TILE, MXU_TILE)),
                _const_spec((1, S5_WIDTH)), _const_spec((N_SCAN_TABLES, SUBLANES, S5_MODES))]
    sds = lambda w, dt: jax.ShapeDtypeStruct((n, w), dt)
    out_shape = [sds(D_MODEL, F32), sds(D_MODEL, BF16), sds(D_MODEL, BF16), sds(GLA_VAL, BF16), sds(S5_WIDTH, BF16),
                 jax.ShapeDtypeStruct(st0.shape, F32), jax.ShapeDtypeStruct(s50.shape, F32)]
    out_specs = [row_cur(D_MODEL), row_cur(D_MODEL), row_cur(D_MODEL), row_prev(GLA_VAL), row_prev(S5_WIDTH),
                 st_spec, s5_spec]
    scratch = [pltpu.VMEM((tile, GLA_KEY), F32), pltpu.VMEM((tile, GLA_KEY), F32), pltpu.VMEM((tile, GLA_KEY), F32),
               pltpu.VMEM((tile, GLA_VAL), BF16), pltpu.VMEM((tile, GLA_VAL), F32),
               pltpu.VMEM((S5_SLABS, tile, LANES), F32),
               pltpu.VMEM((GLA_HEADS // 2, 2 * GLA_DV, LANES), F32), pltpu.VMEM((SUBLANES, 2 * S5_MODES), F32),
               pltpu.VMEM((tile, GLA_VAL), F32), pltpu.VMEM((S5_SLABS, tile, LANES), F32)]
    return pl.pallas_call(
        functools.partial(_fm_kernel, tile=tile, sub=sub, seq_tiles=seq_tiles, streams=streams),
        out_shape=out_shape,
        grid=(ntiles + 1,),
        in_specs=in_specs,
        out_specs=out_specs,
        scratch_shapes=scratch,
        compiler_params=pltpu.CompilerParams(dimension_semantics=("arbitrary",), vmem_limit_bytes=VMEM_LIMIT),
        name="front_mix",
    )(x2d, p['g_ffn1'], p['wg1'], p['wu1'], p['wd1'], p['g_mix'], p['w_main'], p['w_ga'], p['w_gu'], p['b_gate'],
      st0, s50, p['g_gla_head'], *p['s5_tiles'], p['s5_d'], p['tab'])


def _back_kernel(h_ref, og_ref, g5_ref, gmg_ref, gms_ref, wgo_ref, wa_ref, wb_ref, wso_ref, wo_ref,
                 g2_ref, wg_ref, wu_ref, wd_ref, gf_ref, y_ref, *, sub):
    for r0 in range(0, h_ref.shape[0], sub):
        rows = slice(r0, r0 + sub)
        y_gla = _dot(og_ref[rows, :], wgo_ref[...])
        g5 = g5_ref[rows, :]
        glu = _dot(g5, wa_ref[...]) * _sigmoid(_dot(g5, wb_ref[...]))
        y_s5 = _dot(glu.astype(BF16), wso_ref[...])
        m = _sigmoid(gmg_ref[rows, :].astype(F32)) * y_gla + _sigmoid(gms_ref[rows, :].astype(F32)) * y_s5
        h = h_ref[rows, :] + _dot(m.astype(BF16), wo_ref[...])
        hn = _rms(h, g2_ref[...]).astype(BF16)
        h = h + 0.5 * _swiglu(hn, wg_ref, wu_ref, wd_ref)
        y_ref[rows, :] = _rms(h, gf_ref[...])


def _back(h, og, g5, gmg, gms, p, tm, sub):
    n = h.shape[0]
    row = lambda w: pl.BlockSpec((tm, w), lambda i: (i, 0))
    in_specs = [row(D_MODEL), row(GLA_VAL), row(S5_WIDTH), row(D_MODEL), row(D_MODEL),
                _const_spec((GLA_VAL, D_MODEL)), _const_spec((S5_WIDTH, S5_WIDTH)), _const_spec((S5_WIDTH, S5_WIDTH)),
                _const_spec((S5_WIDTH, D_MODEL)), _const_spec((D_MODEL, D_MODEL)), _const_spec((1, D_MODEL)),
                _const_spec((D_MODEL, D_FF)), _const_spec((D_MODEL, D_FF)), _const_spec((D_FF, D_MODEL)),
                _const_spec((1, D_MODEL))]
    return pl.pallas_call(
        functools.partial(_back_kernel, sub=sub),
        out_shape=jax.ShapeDtypeStruct((n, D_MODEL), F32),
        grid=(n // tm,),
        in_specs=in_specs,
        out_specs=row(D_MODEL),
        compiler_params=pltpu.CompilerParams(dimension_semantics=("parallel",), vmem_limit_bytes=VMEM_LIMIT),
        name="back",
    )(h, og, g5, gmg, gms, p['w_gla_out'], p['w_glu_a'], p['w_glu_b'], p['w_s5_out'], p['w_out'],
      p['g_ffn2'], p['wg2'], p['wu2'], p['wd2'], p['g_final'])


def _gla_state_in(s, nb):
    st = jnp.swapaxes(s.astype(F32), 2, 3)
    z = jnp.zeros_like(st)
    sel = (jnp.arange(GLA_HEADS) % 2 == 0)[None, :, None, None]
    st = jnp.where(sel, jnp.concatenate([st, z], axis=-1), jnp.concatenate([z, st], axis=-1))
    return st.reshape(nb, -1, 2 * GLA_DV, LANES)


def _gla_state_out(st, bsz):
    st = st.reshape(bsz, GLA_HEADS, GLA_DV, LANES)
    sel = (jnp.arange(GLA_HEADS) % 2 == 0)[None, :, None, None]
    return jnp.swapaxes(jnp.where(sel, st[..., :GLA_DK], st[..., GLA_DK:]), 2, 3)


def _s5_state_in(x0, nb):
    x = x0.astype(F32).reshape(x0.shape[0], S5_MODES, 2)
    return jnp.concatenate([x[..., 0], x[..., 1]], axis=-1).reshape(nb, -1, 2 * S5_MODES)


def _s5_state_out(x, bsz):
    x = x.reshape(bsz, 2, S5_MODES)
    return jnp.stack([x[:, 0], x[:, 1]], axis=-1).reshape(bsz, S5_GROUPS, S5_STATE, 2)


def _layer(x, s_gla, s_s5, p, tile, sub, streams):
    bsz, t, _ = x.shape
    n = bsz * t
    seq_tiles = 1 if streams else t // tile
    nblk = n // (tile * seq_tiles)
    h, gmg, gms, og, g5, st, s5 = _front_mix(x.reshape(n, D_MODEL), _gla_state_in(s_gla, nblk),
                                             _s5_state_in(s_s5, nblk), p, tile, sub, seq_tiles, streams)
    y = _back(h, og, g5, gmg, gms, p, tile, sub)
    return y.reshape(bsz, t, D_MODEL), _gla_state_out(st, bsz), _s5_state_out(s5, bsz)


def kernel(x_prompt, x_sample, state_gla, state_s5, g_ffn1, w_ffn1_gate, w_ffn1_up, w_ffn1_down, g_mix, w_in, w_gate_up, b_gate, g_gla_head, w_gla_out, s5_lam_re, s5_lam_im, s5_log_dt, s5_b_re, s5_b_im, s5_c_re, s5_c_im, s5_d, w_glu_a, w_glu_b, w_s5_out, w_out, g_ffn2, w_ffn2_gate, w_ffn2_up, w_ffn2_down, g_final):
    vec = lambda a: a.astype(F32).reshape(1, -1)
    sizes = (GLA_KEY, GLA_KEY, GLA_VAL, GLA_VAL, GLA_GATE_RANK, S5_WIDTH, D_MODEL, D_MODEL)
    offs = [sum(sizes[:i]) for i in range(len(sizes) + 1)]
    wq, wk, wv, wr, wga, wu5, wgg, wgs = [w_in[:, offs[i]:offs[i + 1]] for i in range(len(sizes))]
    *s5_tiles, tab = _s5_prep(s5_lam_re, s5_lam_im, s5_log_dt, s5_b_re, s5_b_im, s5_c_re, s5_c_im)
    p = {
        'g_ffn1': vec(g_ffn1), 'wg1': w_ffn1_gate.astype(BF16), 'wu1': w_ffn1_up.astype(BF16),
        'wd1': w_ffn1_down.astype(BF16), 'g_mix': vec(g_mix),
        'w_main': jnp.concatenate([wq, wk, wv, wr, wu5, wgg, wgs], axis=1).astype(BF16),
        'w_ga': jnp.pad(wga, ((0, 0), (0, LANES - GLA_GATE_RANK))).astype(BF16),
        'w_gu': jnp.pad(w_gate_up, ((0, LANES - GLA_GATE_RANK), (0, 0))).astype(BF16),
        'b_gate': vec(b_gate), 'g_gla_head': vec(g_gla_head),
        's5_tiles': s5_tiles, 'tab': tab, 's5_d': vec(s5_d),
        'w_gla_out': w_gla_out.astype(BF16), 'w_glu_a': w_glu_a.astype(BF16), 'w_glu_b': w_glu_b.astype(BF16),
        'w_s5_out': w_s5_out.astype(BF16), 'w_out': w_out.astype(BF16),
        'g_ffn2': vec(g_ffn2), 'wg2': w_ffn2_gate.astype(BF16), 'wu2': w_ffn2_up.astype(BF16),
        'wd2': w_ffn2_down.astype(BF16), 'g_final': vec(g_final),
    }
    bp = x_prompt.shape[0]
    zero_gla = jnp.zeros((bp, GLA_HEADS, GLA_DK, GLA_DV), F32)
    zero_s5 = jnp.zeros((bp, S5_GROUPS, S5_STATE, 2), F32)
    y_p, gla_p, s5_p = _layer(x_prompt, zero_gla, zero_s5, p, tile=512, sub=256, streams=False)
    y_s, gla_s, s5_s = _layer(x_sample, state_gla, state_s5, p, tile=256, sub=128, streams=True)
    return (y_p, y_s, gla_p, s5_p, gla_s, s5_s)
```

```python
import functools
import math

import jax
import jax.numpy as jnp
from jax import lax
from jax.experimental import pallas as pl
from jax.experimental.pallas import tpu as pltpu

F32 = jnp.float32
BF16 = jnp.bfloat16

D_MODEL = 1024
D_FF = 2816
CHUNK = 64
SUB = 16
GLA_HEADS = 4
GLA_DK = 64
GLA_DV = 128
GLA_KEY = GLA_HEADS * GLA_DK
GLA_VAL = GLA_HEADS * GLA_DV
GLA_GATE_RANK = 16
GLA_GATE_TAU = 16.0
S5_WIDTH = 512
S5_GROUP = 16
S5_GROUPS = 32
S5_STATE = 64
S5_MODES = S5_GROUPS * S5_STATE
EPS = 1e-6
SUBLANES = 8
LANES = 128
MXU_TILE = 256
S5_STEP = SUBLANES
S5_PAIRS = S5_GROUPS // 2
PAIR_CH = 2 * S5_GROUP
PAIR_MODES = 2 * S5_STATE
S5_SLABS = S5_WIDTH // LANES
assert (S5_SLABS, LANES) == (GLA_HEADS, GLA_DV)
N_SCAN_TABLES = 8
EXP_CLAMP = 80.0
FF_CHUNK = 512
VMEM_LIMIT = 62 * 1024 * 1024

_OQ, _OK, _OV, _OR, _OU, _OGG, _OGS = 0, 256, 512, 1024, 1536, 2048, 3072
W_MAIN_COLS = 4096


def _dot(a, b):
    return jnp.dot(a, b, preferred_element_type=F32)


def _dot_nt(a, b):
    return lax.dot_general(a, b, (((1,), (1,)), ((), ())), preferred_element_type=F32)


def _split_dot(a, b):
    a1 = a.astype(BF16)
    a2 = (a - a1.astype(F32)).astype(BF16)
    b1 = b.astype(BF16)
    b2 = (b - b1.astype(F32)).astype(BF16)
    return _dot(a1, b1) + _dot(a1, b2) + _dot(a2, b1)


def _cmul(ar, ai, br, bi):
    return ar * br - ai * bi, ar * bi + ai * br


def _sigmoid(x):
    return 1.0 / (1.0 + jnp.exp(-x))


def _rms(x, g):
    ms = jnp.mean(x * x, axis=-1, keepdims=True)
    return x * lax.rsqrt(ms + EPS) * g


def _ff_chunks():
    out, c0 = [], 0
    while c0 < D_FF:
        n = min(FF_CHUNK, D_FF - c0)
        out.append((c0, n))
        c0 += n
    return out


def _swiglu_steps(xns, wg_ref, wu_ref, wd_ref, accs):
    for c0, n in _ff_chunks():
        gus = [(_dot(xn, wg_ref[:, c0:c0 + n]), _dot(xn, wu_ref[:, c0:c0 + n])) for xn in xns]
        for j, (g, u) in enumerate(gus):
            part = _dot((g * _sigmoid(g) * u).astype(BF16), wd_ref[c0:c0 + n, :])
            accs[j] = part if accs[j] is None else accs[j] + part
        yield


def _const_spec(shape):
    nd = len(shape)
    return pl.BlockSpec(shape, lambda *_: (0,) * nd, pipeline_mode=pl.Buffered(1))


def _front_stages(x_ref, blocks, g1_ref, wg_ref, wu_ref, wd_ref, gmix_ref, win_ref, wga_ref, wgu_ref, bg_ref, sink):
    xs = [x_ref[rows, :] for rows in blocks]
    accs = [None] * len(blocks)
    for _ in _swiglu_steps([_rms(x, g1_ref[...]).astype(BF16) for x in xs], wg_ref, wu_ref, wd_ref, accs):
        yield
    hs = [x + 0.5 * acc for x, acc in zip(xs, accs)]
    uns = [_rms(h, gmix_ref[...]).astype(BF16) for h in hs]
    yield
    for rows, h, un in zip(blocks, hs, uns):
        def proj(c0, n, un=un):
            return _dot(un, win_ref[:, c0:c0 + n])

        ga = _dot(un, wga_ref[...])
        pre = _dot(ga.astype(BF16), wgu_ref[...]) + bg_ref[...]
        log_sig = jnp.minimum(pre, 0.0) - jnp.log(1.0 + jnp.exp(-jnp.abs(pre)))
        sink(rows, dict(h=h, q=proj(_OQ, GLA_KEY) * (GLA_DK ** -0.5), k=proj(_OK, GLA_KEY), v=proj(_OV, GLA_VAL),
                        r=proj(_OR, GLA_VAL), u5=proj(_OU, S5_WIDTH), gmg=proj(_OGG, D_MODEL),
                        gms=proj(_OGS, D_MODEL), la=log_sig * (1.0 / GLA_GATE_TAU)))


def _shift_lanes_256(x, s):
    lane = lax.broadcasted_iota(jnp.int32, (1, LANES), 1)
    lo, hi = x[:, :LANES], x[:, LANES:]
    if s == 0:
        return x
    if s >= LANES:
        r = s - LANES
        moved = lo if r == 0 else jnp.where(lane >= r, pltpu.roll(lo, r, axis=1), 0.0)
        return jnp.concatenate([jnp.zeros_like(lo), moved], axis=1)
    rl, rh = pltpu.roll(lo, s, axis=1), pltpu.roll(hi, s, axis=1)
    return jnp.concatenate([jnp.where(lane >= s, rl, 0.0), jnp.where(lane >= s, rh, rl)], axis=1)


def _s5_prep_kernel(lrc_ref, lic_ref, ldtc_ref, lrr_ref, lir_ref, ldtr_ref, br_ref, bi_ref, cr_ref, ci_ref,
                    w1r_ref, w1i_ref, w3xr_ref, w3xi_ref, w3u_ref, tab_ref):
    lr, li = lrc_ref[...], lic_ref[...]
    dt = jnp.exp(ldtc_ref[...])
    ar, th = lr * dt, li * dt

    def powers(kvec):
        mag = jnp.exp(ar * kvec)
        ang = th * kvec
        return mag * jnp.cos(ang), mag * jnp.sin(ang)

    lane = lax.broadcasted_iota(jnp.int32, (1, MXU_TILE), 1)
    step = lax.shift_right_logical(lane, 5)
    lane_group = lax.shift_right_logical(lane, 4) & 1
    mode = lax.broadcasted_iota(jnp.int32, (S5_MODES, 1), 0)
    own = lane_group == (lax.shift_right_logical(mode, 6) & 1)
    chan = lax.broadcasted_iota(jnp.int32, (S5_GROUP, 1), 0)
    rep = jnp.where((lane & (S5_GROUP - 1)) == chan, 1.0, 0.0)
    b_r, b_i = _split_dot(br_ref[...], rep), _split_dot(bi_ref[...], rep)
    c_r, c_i = _split_dot(cr_ref[...], rep), _split_dot(ci_ref[...], rep)

    ab_re, ab_im = powers(jnp.ones((1, 1), F32))
    nr, ni = ab_re - 1.0, ab_im
    den = lr * lr + li * li
    f_re = (nr * lr + ni * li) / den
    f_im = (ni * lr - nr * li) / den
    bbr, bbi = _cmul(f_re, f_im, b_r, b_i)

    p7r, p7i = powers((S5_STEP - 1 - step).astype(F32))
    w1r, w1i = _cmul(p7r, p7i, bbr, bbi)
    w1r_ref[...] = jnp.where(own, w1r, 0.0).astype(BF16)
    w1i_ref[...] = jnp.where(own, w1i, 0.0).astype(BF16)
    p1r, p1i = powers((step + 1).astype(F32))
    xr, xi = _cmul(p1r, p1i, c_r, c_i)
    w3xr_ref[...] = jnp.where(own, xr, 0.0).astype(BF16)
    w3xi_ref[...] = jnp.where(own, -xi, 0.0).astype(BF16)

    p0r, p0i = powers(step.astype(F32))
    car, cai = _cmul(p0r, p0i, c_r, c_i)
    car, cai = jnp.where(own, car, 0.0), jnp.where(own, cai, 0.0)
    lane_w = lax.broadcasted_iota(jnp.int32, (1, S5_WIDTH), 1)
    rep_w = jnp.where((lane_w & (S5_GROUP - 1)) == chan, 1.0, 0.0)
    own_w = lax.shift_right_logical(lane_w, 4) == lax.shift_right_logical(mode, 6)
    dbr, dbi = _cmul(f_re, f_im, _split_dot(br_ref[...], rep_w), _split_dot(bi_ref[...], rep_w))
    dbr, dbi = jnp.where(own_w, dbr, 0.0), jnp.where(own_w, dbi, 0.0)
    taps = _split_dot(dbr.T, car) - _split_dot(dbi.T, cai)
    for i in range(S5_STEP):
        blk = _shift_lanes_256(taps, PAIR_CH * i)
        w3u_ref[:, PAIR_CH * i:PAIR_CH * (i + 1), :] = blk.reshape(S5_PAIRS, PAIR_CH, MXU_TILE).astype(BF16)

    arr = lrr_ref[...] * jnp.exp(ldtr_ref[...])
    thr = lir_ref[...] * jnp.exp(ldtr_ref[...])
    pos = lax.broadcasted_iota(jnp.int32, (SUBLANES, 1), 0)

    def powers_rows(kcol):
        mag = jnp.exp(arr * kcol)
        ang = thr * kcol
        return mag * jnp.cos(ang), mag * jnp.sin(ang)

    for s, kstep in enumerate((1, 2, 4)):
        kr, ki = powers_rows(jnp.full((1, 1), float(S5_STEP * kstep), F32))
        tab_ref[2 * s] = jnp.where(pos >= kstep, kr, 0.0)
        tab_ref[2 * s + 1] = jnp.where(pos >= kstep, ki, 0.0)
    tab_ref[6], tab_ref[7] = powers_rows(((pos + 1) * S5_STEP).astype(F32))


def _s5_prep(lam_re, lam_im, log_dt, b_re, b_im, c_re, c_im):
    m = S5_MODES
    ldt = jnp.broadcast_to(log_dt.astype(F32)[:, None], (S5_GROUPS, S5_STATE))
    col = lambda a: a.astype(F32).reshape(m, 1)
    rowv = lambda a: a.astype(F32).reshape(1, m)
    bmat = lambda a: a.astype(F32).reshape(m, S5_GROUP)
    cmat = lambda a: a.astype(F32).transpose(0, 2, 1).reshape(m, S5_GROUP)
    tile = jax.ShapeDtypeStruct((m, MXU_TILE), BF16)
    out_shape = [tile] * 4 + [jax.ShapeDtypeStruct((S5_PAIRS, MXU_TILE, MXU_TILE), BF16),
                              jax.ShapeDtypeStruct((N_SCAN_TABLES, SUBLANES, m), F32)]
    w1r, w1i, w3xr, w3xi, w3u, tab = pl.pallas_call(
        _s5_prep_kernel,
        out_shape=out_shape,
        compiler_params=pltpu.CompilerParams(vmem_limit_bytes=VMEM_LIMIT),
        name="s5_prep",
    )(col(lam_re), col(lam_im), col(ldt), rowv(lam_re), rowv(lam_im), rowv(ldt),
      bmat(b_re), bmat(b_im), cmat(c_re), cmat(c_im))
    per_pair = lambda a: a.reshape(S5_PAIRS, PAIR_MODES, MXU_TILE)
    return per_pair(w1r), per_pair(w1i), per_pair(w3xr), per_pair(w3xi), w3u, tab


def _lane_window(lo):
    lane = lax.broadcasted_iota(jnp.int32, (1, LANES), 1)
    return (lane >= lo) & (lane < lo + PAIR_CH)


def _merge_windows(pieces):
    acc = pieces[0]
    for k in range(1, len(pieces)):
        acc = jnp.where(_lane_window(PAIR_CH * k), pieces[k], acc)
    return acc


def _rolled(cache, key, make, shift):
    if (key, shift) not in cache:
        x = make()
        cache[(key, shift)] = x if shift == 0 else pltpu.roll(x, shift, axis=1)
    return cache[(key, shift)]


def _s5_regroup_in(step_slab):
    per_tile = LANES // PAIR_CH
    cache, outs = {}, []
    for q in range(S5_PAIRS):
        halves = []
        for m in range(MXU_TILE // LANES):
            pieces = []
            for i in range(per_tile * m, per_tile * (m + 1)):
                c = q // per_tile
                pieces.append(_rolled(cache, (i, c), lambda i=i, c=c: step_slab(i, c),
                                      PAIR_CH * ((i - q) % per_tile)))
            halves.append(_merge_windows(pieces))
        outs.append(jnp.concatenate(halves, axis=1))
    return outs


def _s5_regroup_out(ys, j):
    per_tile = LANES // PAIR_CH
    m = j // per_tile
    slabs = []
    for c in range(S5_SLABS):
        pieces = [ys[q][:, LANES * m:LANES * (m + 1)] for q in range(per_tile * c, per_tile * (c + 1))]
        shifts = [PAIR_CH * ((q - j) % per_tile) for q in range(per_tile * c, per_tile * (c + 1))]
        slabs.append(_merge_windows([x if sh == 0 else pltpu.roll(x, sh, axis=1) for x, sh in zip(pieces, shifts)]))
    return slabs


def _mix_stages(q_ref, k_ref, la_ref, v_ref, r_ref, u5_ref, st0_ref, s50_ref, gh_ref,
                w1r_ref, w1i_ref, w3xr_ref, w3xi_ref, w3u_ref, d_ref, tab_ref,
                og_ref, g5_ref, sto_ref, s5o_ref,
                st_sc, car_sc, slab_sc, *, tb, streams, first):
    nm = S5_MODES
    nch = tb // CHUNK
    crow = lambda c: slice(c * CHUNK, (c + 1) * CHUNK)
    ti = lax.broadcasted_iota(jnp.int32, (CHUNK, CHUNK), 0)
    si = lax.broadcasted_iota(jnp.int32, (CHUNK, CHUNK), 1)
    tril = jnp.where(si <= ti, 1.0, 0.0).astype(BF16)
    lane = lax.broadcasted_iota(jnp.int32, (1, GLA_KEY), 1) % LANES
    head_mask = (lane < GLA_DK, lane >= GLA_DK)
    nsub = CHUNK // SUB
    npairs = GLA_HEADS // 2
    pt = lax.broadcasted_iota(jnp.int32, (2 * CHUNK, LANES), 0) & (CHUNK - 1)
    ps = lax.broadcasted_iota(jnp.int32, (2 * CHUNK, LANES), 1)
    causal_pair = ps <= pt
    sr = lax.broadcasted_iota(jnp.int32, (2 * GLA_DV, LANES), 0)
    sl = lax.broadcasted_iota(jnp.int32, (2 * GLA_DV, LANES), 1)
    own_head = (sr < GLA_DV) == (sl < GLA_DK)

    def row_block(x, i):
        parts = [jnp.zeros((SUB, x.shape[1]), x.dtype)] * nsub
        parts[i] = x[SUB * i:SUB * (i + 1)]
        return jnp.concatenate(parts, axis=0)

    bs = []
    for c in range(nch):
        la = la_ref[crow(c), :]
        hi = la.astype(BF16)
        lo = (la - hi.astype(F32)).astype(BF16)
        bs.append(_dot(tril, hi) + _dot(tril, lo))
    tg = tb // S5_STEP
    ng = tg // SUBLANES
    step_slab = lambda i, c: u5_ref[c, pl.ds(i, tg, stride=S5_STEP), :]
    uq = [x.astype(BF16) for x in _s5_regroup_in(step_slab)]
    w = [_dot_nt(uq[q], jnp.concatenate([w1r_ref[q], w1i_ref[q]], axis=0)) for q in range(S5_PAIRS)]
    skip = [[d_ref[:, LANES * c:LANES * (c + 1)] * step_slab(j, c) for c in range(S5_SLABS)] for j in range(S5_STEP)]
    yield

    pms, qds, upds, els = [], [], [], []
    for c in range(nch):
        b = bs[c]
        refs = [jnp.zeros((1, GLA_KEY), F32)] + [b[SUB * i - 1:SUB * i, :] for i in range(1, nsub)]
        rb = jnp.concatenate([jnp.broadcast_to(x, (SUB, GLA_KEY)) for x in refs], axis=0)
        q, k = q_ref[crow(c), :], k_ref[crow(c), :]
        b_last = b[CHUNK - 1:CHUNK, :]
        qt = q * jnp.exp(b - rb)
        qt = [jnp.where(m, qt, 0.0).astype(BF16) for m in head_mask]
        qds.append((q * jnp.exp(b)).astype(BF16))
        kd = (k * jnp.exp(b_last - b)).astype(BF16)
        els.append(jnp.exp(b_last))
        kts = [(k * jnp.exp(jnp.minimum(x - b, EXP_CLAMP))).astype(BF16) for x in refs]
        pm_c, upd_c = [], []
        for pp in range(npairs):
            pc = slice(LANES * pp, LANES * (pp + 1))
            lhs = jnp.concatenate(
                [jnp.concatenate([row_block(qt[e][:, pc], i) for i in range(nsub)], axis=1) for e in (0, 1)], axis=0)
            rhs = jnp.concatenate([kts[i][:, pc] for i in range(nsub)], axis=1)
            rhs = jnp.concatenate([rhs, jnp.zeros_like(rhs)], axis=0)
            pm_c.append(jnp.where(causal_pair, _dot_nt(lhs, rhs), 0.0).astype(BF16))
            vt = v_ref[crow(c), 2 * GLA_DV * pp:2 * GLA_DV * (pp + 1)].astype(F32).T.astype(BF16)
            upd_c.append(_dot(vt, kd[:, pc]))
        pms.append(pm_c)
        upds.append(upd_c)
    r = r_ref[...]
    silu_r = r * _sigmoid(r)
    yield

    for pp in range(npairs):
        pc = slice(LANES * pp, LANES * (pp + 1))
        vc = slice(2 * GLA_DV * pp, 2 * GLA_DV * (pp + 1))
        if not streams:
            st = jnp.where(first, st0_ref[0, pp], st_sc[pp])
        for c in range(nch):
            if streams:
                st = st0_ref[0, c * npairs + pp]
            vp = v_ref[crow(c), vc]
            z = jnp.zeros((CHUNK, GLA_DV), BF16)
            zz = jnp.zeros((CHUNK, 2 * GLA_DV), BF16)
            v_diag = jnp.concatenate([jnp.concatenate([vp[:, :GLA_DV], z], axis=1), zz,
                                      jnp.concatenate([z, vp[:, GLA_DV:]], axis=1), zz], axis=0)
            pm = pms[c][pp]
            o = (_dot(jnp.concatenate([pm[:CHUNK], pm[CHUNK:]], axis=1), v_diag)
                 + _dot_nt(qds[c][:, pc], st.astype(BF16)))
            slab_sc[2 * pp, crow(c), :] = o[:, :GLA_DV]
            slab_sc[2 * pp + 1, crow(c), :] = o[:, GLA_DV:]
            st = jnp.where(own_head, st * els[c][:, pc] + upds[c][pp], 0.0)
            if streams:
                sto_ref[0, c * npairs + pp] = st
        if not streams:
            st_sc[pp] = st
    if not streams:
        sto_ref[0] = st_sc[...]
    yield

    normed = []
    for h in range(GLA_HEADS):
        oh = slab_sc[h]
        normed.append(oh * lax.rsqrt(jnp.mean(oh * oh, axis=-1, keepdims=True) + EPS))
    og_ref[...] = (jnp.concatenate(normed, axis=1) * gh_ref[...] * silu_r).astype(BF16)
    xr = jnp.concatenate([x[:, :LANES] for x in w], axis=1).reshape(ng, SUBLANES, nm)
    xi = jnp.concatenate([x[:, LANES:] for x in w], axis=1).reshape(ng, SUBLANES, nm)
    for s, kstep in enumerate((1, 2, 4)):
        ar, ai = tab_ref[2 * s], tab_ref[2 * s + 1]
        sre = pltpu.roll(xr, kstep, axis=1)
        sim = pltpu.roll(xi, kstep, axis=1)
        xr, xi = xr + ar * sre - ai * sim, xi + ar * sim + ai * sre
    pr, pi = tab_ref[6], tab_ref[7]
    first_row = lax.broadcasted_iota(jnp.int32, (SUBLANES, nm), 0) == 0
    if not streams:
        cr = jnp.where(first, jnp.broadcast_to(s50_ref[0, :, :nm], (SUBLANES, nm)), car_sc[:, :nm])
        ci = jnp.where(first, jnp.broadcast_to(s50_ref[0, :, nm:], (SUBLANES, nm)), car_sc[:, nm:])
    prev_r, prev_i = [], []
    for g in range(ng):
        if streams:
            cr = jnp.broadcast_to(s50_ref[0, g:g + 1, :nm], (SUBLANES, nm))
            ci = jnp.broadcast_to(s50_ref[0, g:g + 1, nm:], (SUBLANES, nm))
        yr = xr[g] + pr * cr - pi * ci
        yi = xi[g] + pr * ci + pi * cr
        prev_r.append(jnp.where(first_row, cr, pltpu.roll(yr, 1, axis=0)))
        prev_i.append(jnp.where(first_row, ci, pltpu.roll(yi, 1, axis=0)))
        if streams:
            s5o_ref[0, g:g + 1, :nm] = yr[SUBLANES - 1:SUBLANES]
            s5o_ref[0, g:g + 1, nm:] = yi[SUBLANES - 1:SUBLANES]
        else:
            cr = jnp.broadcast_to(yr[SUBLANES - 1:SUBLANES], (SUBLANES, nm))
            ci = jnp.broadcast_to(yi[SUBLANES - 1:SUBLANES], (SUBLANES, nm))
    if not streams:
        car_sc[:, :nm] = cr
        car_sc[:, nm:] = ci
        s5o_ref[0] = car_sc[0:1, :]
    prev_r = jnp.concatenate(prev_r, axis=0).astype(BF16)
    prev_i = jnp.concatenate(prev_i, axis=0).astype(BF16)
    ys = []
    for q in range(S5_PAIRS):
        ms = slice(PAIR_MODES * q, PAIR_MODES * (q + 1))
        lhs = jnp.concatenate([prev_r[:, ms], prev_i[:, ms], uq[q]], axis=1)
        ys.append(_dot(lhs, jnp.concatenate([w3xr_ref[q], w3xi_ref[q], w3u_ref[q]], axis=0)))
    yield

    for j in range(S5_STEP):
        conv = _s5_regroup_out(ys, j)
        for c in range(S5_SLABS):
            y = conv[c] + skip[j][c]
            inner = math.sqrt(2.0 / math.pi) * (y + 0.044715 * (y * y * y))
            slab_sc[c, pl.ds(j, tg, stride=S5_STEP), :] = 0.5 * y * (1.0 + jnp.tanh(inner))
        if j % 3 == 2:
            yield
    g5_ref[...] = jnp.concatenate([slab_sc[c] for c in range(S5_SLABS)], axis=1).astype(BF16)


def _fm_kernel(x_ref, g1_ref, wg_ref, wu_ref, wd_ref, gmix_ref, win_ref, wga_ref, wgu_ref, bg_ref,
               st0_ref, s50_ref, gh_ref, w1r_ref, w1i_ref, w3xr_ref, w3xi_ref, w3u_ref, d_ref, tab_ref,
               h_ref, gmg_ref, gms_ref, og_ref, g5_ref, sto_ref, s5o_ref,
               q_sc, k_sc, la_sc, v_sc, r_sc, u5_sc, st_sc, car_sc, slab_sc, *, tile, sub, seq_tiles, streams):
    s = pl.program_id(0)

    @pl.when(s == 0)
    def _():
        for ref in (q_sc, k_sc, la_sc, v_sc, r_sc, u5_sc, st_sc, car_sc):
            ref[...] = jnp.zeros(ref.shape, ref.dtype)

    first = lax.rem(jnp.maximum(s - 1, 0), seq_tiles) == 0
    mix = _mix_stages(q_sc, k_sc, la_sc, v_sc, r_sc, u5_sc, st0_ref, s50_ref, gh_ref,
                      w1r_ref, w1i_ref, w3xr_ref, w3xi_ref, w3u_ref, d_ref, tab_ref,
                      og_ref, g5_ref, sto_ref, s5o_ref, st_sc, car_sc, slab_sc,
                      tb=tile, streams=streams, first=first)

    def sink(rows, f):
        h_ref[rows, :] = f['h']
        gmg_ref[rows, :] = f['gmg'].astype(BF16)
        gms_ref[rows, :] = f['gms'].astype(BF16)
        q_sc[rows, :] = f['q']
        k_sc[rows, :] = f['k']
        la_sc[rows, :] = f['la']
        v_sc[rows, :] = f['v'].astype(BF16)
        r_sc[rows, :] = f['r']
        for c in range(S5_SLABS):
            u5_sc[c, rows, :] = f['u5'][:, LANES * c:LANES * (c + 1)]

    front = _front_stages(x_ref, [slice(r0, r0 + sub) for r0 in range(0, tile, sub)], g1_ref, wg_ref, wu_ref, wd_ref,
                          gmix_ref, win_ref, wga_ref, wgu_ref, bg_ref, sink)
    for _ in front:
        next(mix, None)
    for _ in mix:
        pass


def _front_mix(x2d, st0, s50, p, tile, sub, seq_tiles, streams):
    n = x2d.shape[0]
    ntiles = n // tile
    cur = lambda s: jnp.minimum(s, ntiles - 1)
    prev = lambda s: jnp.maximum(s - 1, 0)
    row_cur = lambda w: pl.BlockSpec((tile, w), lambda s: (cur(s), 0))
    row_prev = lambda w: pl.BlockSpec((tile, w), lambda s: (prev(s), 0))
    st_spec = pl.BlockSpec((1,) + st0.shape[1:], lambda s: (prev(s) // seq_tiles, 0, 0, 0))
    s5_spec = pl.BlockSpec((1,) + s50.shape[1:], lambda s: (prev(s) // seq_tiles, 0, 0))
    pair_tile = _const_spec((S5_PAIRS, PAIR_MODES, MXU_TILE))
    in_specs = [row_cur(D_MODEL), _const_spec((1, D_MODEL)),
                _const_spec((D_MODEL, D_FF)), _const_spec((D_MODEL, D_FF)), _const_spec((D_FF, D_MODEL)),
                _const_spec((1, D_MODEL)), _const_spec((D_MODEL, W_MAIN_COLS)),
                _const_spec((D_MODEL, LANES)), _const_spec((LANES, GLA_KEY)), _const_spec((1, GLA_KEY)),
                st_spec, s5_spec, _const_spec((1, GLA_VAL)),
                pair_tile, pair_tile, pair_tile, pair_tile, _const_spec((S5_PAIRS, MXU_TILE, MXU_TILE)),
                _const_spec((1, S5_WIDTH)), _const_spec((N_SCAN_TABLES, SUBLANES, S5_MODES))]
    sds = lambda w, dt: jax.ShapeDtypeStruct((n, w), dt)
    out_shape = [sds(D_MODEL, F32), sds(D_MODEL, BF16), sds(D_MODEL, BF16), sds(GLA_VAL, BF16), sds(S5_WIDTH, BF16),
                 jax.ShapeDtypeStruct(st0.shape, F32), jax.ShapeDtypeStruct(s50.shape, F32)]
    out_specs = [row_cur(D_MODEL), row_cur(D_MODEL), row_cur(D_MODEL), row_prev(GLA_VAL), row_prev(S5_WIDTH),
                 st_spec, s5_spec]
    scratch = [pltpu.VMEM((tile, GLA_KEY), F32), pltpu.VMEM((tile, GLA_KEY), F32), pltpu.VMEM((tile, GLA_KEY), F32),
               pltpu.VMEM((tile, GLA_VAL), BF16), pltpu.VMEM((tile, GLA_VAL), F32),
               pltpu.VMEM((S5_SLABS, tile, LANES), F32),
               pltpu.VMEM((GLA_HEADS // 2, 2 * GLA_DV, LANES), F32), pltpu.VMEM((SUBLANES, 2 * S5_MODES), F32),
               pltpu.VMEM((S5_SLABS, tile, LANES), F32)]
    return pl.pallas_call(
        functools.partial(_fm_kernel, tile=tile, sub=sub, seq_tiles=seq_tiles, streams=streams),
        out_shape=out_shape,
        grid=(ntiles + 1,),
        in_specs=in_specs,
        out_specs=out_specs,
        scratch_shapes=scratch,
        compiler_params=pltpu.CompilerParams(dimension_semantics=("arbitrary",), vmem_limit_bytes=VMEM_LIMIT),
        name="front_mix",
    )(x2d, p['g_ffn1'], p['wg1'], p['wu1'], p['wd1'], p['g_mix'], p['w_main'], p['w_ga'], p['w_gu'], p['b_gate'],
      st0, s50, p['g_gla_head'], *p['s5_tiles'], p['s5_d'], p['tab'])


def _back_kernel(h_ref, og_ref, g5_ref, gmg_ref, gms_ref, wgo_ref, wa_ref, wb_ref, wso_ref, wo_ref,
                 g2_ref, wg_ref, wu_ref, wd_ref, gf_ref, y_ref, *, sub):
    blocks = [slice(r0, r0 + sub) for r0 in range(0, h_ref.shape[0], sub)]
    y_gla = [_dot(og_ref[rows, :], wgo_ref[...]) for rows in blocks]
    ab = [(_dot(g5_ref[rows, :], wa_ref[...]), _dot(g5_ref[rows, :], wb_ref[...])) for rows in blocks]
    y_s5 = [_dot((a * _sigmoid(b)).astype(BF16), wso_ref[...]) for a, b in ab]
    hs = []
    for rows, yg, ys in zip(blocks, y_gla, y_s5):
        m = _sigmoid(gmg_ref[rows, :].astype(F32)) * yg + _sigmoid(gms_ref[rows, :].astype(F32)) * ys
        hs.append(h_ref[rows, :] + _dot(m.astype(BF16), wo_ref[...]))
    accs = [None] * len(blocks)
    for _ in _swiglu_steps([_rms(h, g2_ref[...]).astype(BF16) for h in hs], wg_ref, wu_ref, wd_ref, accs):
        pass
    for rows, h, acc in zip(blocks, hs, accs):
        y_ref[rows, :] = _rms(h + 0.5 * acc, gf_ref[...])


def _back(h, og, g5, gmg, gms, p, tm, sub):
    n = h.shape[0]
    row = lambda w: pl.BlockSpec((tm, w), lambda i: (i, 0))
    in_specs = [row(D_MODEL), row(GLA_VAL), row(S5_WIDTH), row(D_MODEL), row(D_MODEL),
                _const_spec((GLA_VAL, D_MODEL)), _const_spec((S5_WIDTH, S5_WIDTH)), _const_spec((S5_WIDTH, S5_WIDTH)),
                _const_spec((S5_WIDTH, D_MODEL)), _const_spec((D_MODEL, D_MODEL)), _const_spec((1, D_MODEL)),
                _const_spec((D_MODEL, D_FF)), _const_spec((D_MODEL, D_FF)), _const_spec((D_FF, D_MODEL)),
                _const_spec((1, D_MODEL))]
    return pl.pallas_call(
        functools.partial(_back_kernel, sub=sub),
        out_shape=jax.ShapeDtypeStruct((n, D_MODEL), F32),
        grid=(n // tm,),
        in_specs=in_specs,
        out_specs=row(D_MODEL),
        compiler_params=pltpu.CompilerParams(dimension_semantics=("parallel",), vmem_limit_bytes=VMEM_LIMIT),
        name="back",
    )(h, og, g5, gmg, gms, p['w_gla_out'], p['w_glu_a'], p['w_glu_b'], p['w_s5_out'], p['w_out'],
      p['g_ffn2'], p['wg2'], p['wu2'], p['wd2'], p['g_final'])


def _gla_state_in(s, nb):
    st = jnp.swapaxes(s.astype(F32), 2, 3)
    z = jnp.zeros_like(st)
    sel = (jnp.arange(GLA_HEADS) % 2 == 0)[None, :, None, None]
    st = jnp.where(sel, jnp.concatenate([st, z], axis=-1), jnp.concatenate([z, st], axis=-1))
    return st.reshape(nb, -1, 2 * GLA_DV, LANES)


def _gla_state_out(st, bsz):
    st = st.reshape(bsz, GLA_HEADS, GLA_DV, LANES)
    sel = (jnp.arange(GLA_HEADS) % 2 == 0)[None, :, None, None]
    return jnp.swapaxes(jnp.where(sel, st[..., :GLA_DK], st[..., GLA_DK:]), 2, 3)


def _s5_state_in(x0, nb):
    x = x0.astype(F32).reshape(x0.shape[0], S5_MODES, 2)
    return jnp.concatenate([x[..., 0], x[..., 1]], axis=-1).reshape(nb, -1, 2 * S5_MODES)


def _s5_state_out(x, bsz):
    x = x.reshape(bsz, 2, S5_MODES)
    return jnp.stack([x[:, 0], x[:, 1]], axis=-1).reshape(bsz, S5_GROUPS, S5_STATE, 2)


def _layer(x, s_gla, s_s5, p, tile, sub, streams):
    bsz, t, _ = x.shape
    n = bsz * t
    seq_tiles = 1 if streams else t // tile
    nblk = n // (tile * seq_tiles)
    h, gmg, gms, og, g5, st, s5 = _front_mix(x.reshape(n, D_MODEL), _gla_state_in(s_gla, nblk),
                                             _s5_state_in(s_s5, nblk), p, tile, sub, seq_tiles, streams)
    y = _back(h, og, g5, gmg, gms, p, tile, sub)
    return y.reshape(bsz, t, D_MODEL), _gla_state_out(st, bsz), _s5_state_out(s5, bsz)


def kernel(x_prompt, x_sample, state_gla, state_s5, g_ffn1, w_ffn1_gate, w_ffn1_up, w_ffn1_down, g_mix, w_in, w_gate_up, b_gate, g_gla_head, w_gla_out, s5_lam_re, s5_lam_im, s5_log_dt, s5_b_re, s5_b_im, s5_c_re, s5_c_im, s5_d, w_glu_a, w_glu_b, w_s5_out, w_out, g_ffn2, w_ffn2_gate, w_ffn2_up, w_ffn2_down, g_final):
    vec = lambda a: a.astype(F32).reshape(1, -1)
    sizes = (GLA_KEY, GLA_KEY, GLA_VAL, GLA_VAL, GLA_GATE_RANK, S5_WIDTH, D_MODEL, D_MODEL)
    offs = [sum(sizes[:i]) for i in range(len(sizes) + 1)]
    wq, wk, wv, wr, wga, wu5, wgg, wgs = [w_in[:, offs[i]:offs[i + 1]] for i in range(len(sizes))]
    *s5_tiles, tab = _s5_prep(s5_lam_re, s5_lam_im, s5_log_dt, s5_b_re, s5_b_im, s5_c_re, s5_c_im)
    p = {
        'g_ffn1': vec(g_ffn1), 'wg1': w_ffn1_gate.astype(BF16), 'wu1': w_ffn1_up.astype(BF16),
        'wd1': w_ffn1_down.astype(BF16), 'g_mix': vec(g_mix),
        'w_main': jnp.concatenate([wq, wk, wv, wr, wu5, wgg, wgs], axis=1).astype(BF16),
        'w_ga': jnp.pad(wga, ((0, 0), (0, LANES - GLA_GATE_RANK))).astype(BF16),
        'w_gu': jnp.pad(w_gate_up, ((0, LANES - GLA_GATE_RANK), (0, 0))).astype(BF16),
        'b_gate': vec(b_gate), 'g_gla_head': vec(g_gla_head),
        's5_tiles': s5_tiles, 'tab': tab, 's5_d': vec(s5_d),
        'w_gla_out': w_gla_out.astype(BF16), 'w_glu_a': w_glu_a.astype(BF16), 'w_glu_b': w_glu_b.astype(BF16),
        'w_s5_out': w_s5_out.astype(BF16), 'w_out': w_out.astype(BF16),
        'g_ffn2': vec(g_ffn2), 'wg2': w_ffn2_gate.astype(BF16), 'wu2': w_ffn2_up.astype(BF16),
        'wd2': w_ffn2_down.astype(BF16), 'g_final': vec(g_final),
    }
    bp = x_prompt.shape[0]
    zero_gla = jnp.zeros((bp, GLA_HEADS, GLA_DK, GLA_DV), F32)
    zero_s5 = jnp.zeros((bp, S5_GROUPS, S5_STATE, 2), F32)
    y_p, gla_p, s5_p = _layer(x_prompt, zero_gla, zero_s5, p, tile=512, sub=256, streams=False)
    y_s, gla_s, s5_s = _layer(x_sample, state_gla, state_s5, p, tile=256, sub=128, streams=True)
    return (y_p, y_s, gla_p, s5_p, gla_s, s5_s)
```

```python
import functools
import math

import jax
import jax.numpy as jnp
from jax import lax
from jax.experimental import pallas as pl
from jax.experimental.pallas import tpu as pltpu

F32 = jnp.float32
BF16 = jnp.bfloat16

D_MODEL = 1024
D_FF = 2816
CHUNK = 64
SUB = 16
GLA_HEADS = 4
GLA_DK = 64
GLA_DV = 128
GLA_KEY = GLA_HEADS * GLA_DK
GLA_VAL = GLA_HEADS * GLA_DV
GLA_GATE_RANK = 16
GLA_GATE_TAU = 16.0
S5_WIDTH = 512
S5_GROUP = 16
S5_GROUPS = 32
S5_STATE = 64
S5_MODES = S5_GROUPS * S5_STATE
EPS = 1e-6
SUBLANES = 8
LANES = 128
MXU_TILE = 256
S5_STEP = SUBLANES
S5_PAIRS = S5_GROUPS // 2
PAIR_CH = 2 * S5_GROUP
PAIR_MODES = 2 * S5_STATE
S5_SLABS = S5_WIDTH // LANES
assert (S5_SLABS, LANES) == (GLA_HEADS, GLA_DV)
N_SCAN_TABLES = 8
EXP_CLAMP = 80.0
FF_CHUNK = 256
BACK_TILE = 512
PROMPT_TILE = 512
STREAM_TILE = 256
VMEM_LIMIT = 62 * 1024 * 1024

_OQ, _OK, _OV, _OR, _OU, _OGG, _OGS = 0, 256, 512, 1024, 1536, 2048, 3072
W_MAIN_COLS = 4096


def _dot(a, b):
    return jnp.dot(a, b, preferred_element_type=F32)


def _dot_nt(a, b):
    return lax.dot_general(a, b, (((1,), (1,)), ((), ())), preferred_element_type=F32)


def _split_dot(a, b):
    a1 = a.astype(BF16)
    a2 = (a - a1.astype(F32)).astype(BF16)
    b1 = b.astype(BF16)
    b2 = (b - b1.astype(F32)).astype(BF16)
    return _dot(a1, b1) + _dot(a1, b2) + _dot(a2, b1)


def _split_bmm(a, b):
    bmm = lambda x, y: jnp.einsum('gmk,gkn->gmn', x, y, preferred_element_type=F32)
    a1 = a.astype(BF16)
    a2 = (a - a1.astype(F32)).astype(BF16)
    b1 = b.astype(BF16)
    b2 = (b - b1.astype(F32)).astype(BF16)
    return bmm(a1, b1) + bmm(a1, b2) + bmm(a2, b1)


def _cmul(ar, ai, br, bi):
    return ar * br - ai * bi, ar * bi + ai * br


def _sigmoid(x):
    return 1.0 / (1.0 + jnp.exp(-x))


def _rms(x, g):
    ms = jnp.mean(x * x, axis=-1, keepdims=True)
    return x * lax.rsqrt(ms + EPS) * g


def _ff_chunks():
    out, c0 = [], 0
    while c0 < D_FF:
        n = min(FF_CHUNK, D_FF - c0)
        out.append((c0, n))
        c0 += n
    return out


def _swiglu_steps(xns, wg_ref, wu_ref, wd_ref, accs):
    for c0, n in _ff_chunks():
        gus = [(_dot(xn, wg_ref[:, c0:c0 + n]), _dot(xn, wu_ref[:, c0:c0 + n])) for xn in xns]
        for j, (g, u) in enumerate(gus):
            part = _dot((g * _sigmoid(g) * u).astype(BF16), wd_ref[c0:c0 + n, :])
            accs[j] = part if accs[j] is None else accs[j] + part
        yield


def _const_spec(shape):
    nd = len(shape)
    return pl.BlockSpec(shape, lambda *_: (0,) * nd, pipeline_mode=pl.Buffered(1))


def _front_stages(x_ref, blocks, g1_ref, wg_ref, wu_ref, wd_ref, gmix_ref, win_ref, wga_ref, wgu_ref, bg_ref, sink):
    xs = [x_ref[rows, :] for rows in blocks]
    accs = [None] * len(blocks)
    for _ in _swiglu_steps([_rms(x, g1_ref[...]).astype(BF16) for x in xs], wg_ref, wu_ref, wd_ref, accs):
        yield
    hs = [x + 0.5 * acc for x, acc in zip(xs, accs)]
    uns = [_rms(h, gmix_ref[...]).astype(BF16) for h in hs]
    yield
    proj = lambda un, c0, n: _dot(un, win_ref[:, c0:c0 + n])
    for rows, h, un in zip(blocks, hs, uns):
        sink(rows, dict(h=h, q=proj(un, _OQ, GLA_KEY) * (GLA_DK ** -0.5), k=proj(un, _OK, GLA_KEY),
                        v=proj(un, _OV, GLA_VAL), r=proj(un, _OR, GLA_VAL)))
    yield
    for rows, un in zip(blocks, uns):
        sink(rows, dict(u5=proj(un, _OU, S5_WIDTH), gmg=proj(un, _OGG, D_MODEL)))
    yield
    for rows, un in zip(blocks, uns):
        ga = _dot(un, wga_ref[...])
        pre = _dot(ga.astype(BF16), wgu_ref[...]) + bg_ref[...]
        log_sig = jnp.minimum(pre, 0.0) - jnp.log(1.0 + jnp.exp(-jnp.abs(pre)))
        sink(rows, dict(gms=proj(un, _OGS, D_MODEL), la=log_sig * (1.0 / GLA_GATE_TAU)))


def _shift_lanes_256(x, s):
    lane = lax.broadcasted_iota(jnp.int32, (1, LANES), 1)
    lo, hi = x[:, :LANES], x[:, LANES:]
    if s == 0:
        return x
    if s >= LANES:
        r = s - LANES
        moved = lo if r == 0 else jnp.where(lane >= r, pltpu.roll(lo, r, axis=1), 0.0)
        return jnp.concatenate([jnp.zeros_like(lo), moved], axis=1)
    rl, rh = pltpu.roll(lo, s, axis=1), pltpu.roll(hi, s, axis=1)
    return jnp.concatenate([jnp.where(lane >= s, rl, 0.0), jnp.where(lane >= s, rh, rl)], axis=1)


def _s5_prep_kernel(lrr_ref, lir_ref, ldtr_ref, lrg_ref, lig_ref, ldtg_ref,
                    br_ref, bi_ref, cr_ref, ci_ref, btr_ref, bti_ref,
                    w1r_ref, w1i_ref, w3xr_ref, w3xi_ref, w3u_ref, tab_ref):
    lrr, lir = lrr_ref[...], lir_ref[...]
    arr = lrr * jnp.exp(ldtr_ref[...])
    thr = lir * jnp.exp(ldtr_ref[...])

    def powers_rows(kcol):
        mag = jnp.exp(arr * kcol)
        ang = thr * kcol
        return mag * jnp.cos(ang), mag * jnp.sin(ang)

    kpow = lax.broadcasted_iota(jnp.int32, (2 * SUBLANES, 1), 0)
    pk_r, pk_i = powers_rows(kpow.astype(F32))
    nr, ni = pk_r[1:2] - 1.0, pk_i[1:2]
    den = lrr * lrr + lir * lir
    f_rows = jnp.concatenate([(nr * lrr + ni * lir) / den, (ni * lrr - nr * lir) / den,
                              jnp.zeros((SUBLANES - 2, S5_MODES), F32)], axis=0)
    f_cols = f_rows.T
    f_re, f_im = f_cols[:, 0:1], f_cols[:, 1:2]
    pkt_r, pkt_i = pk_r.T, pk_i.T

    def powers(kvec):
        onehot = jnp.where(kpow == kvec, 1.0, 0.0)
        return _split_dot(pkt_r, onehot), _split_dot(pkt_i, onehot)

    lane = lax.broadcasted_iota(jnp.int32, (1, MXU_TILE), 1)
    step = lax.shift_right_logical(lane, 5)
    lane_group = lax.shift_right_logical(lane, 4) & 1
    mode = lax.broadcasted_iota(jnp.int32, (S5_MODES, 1), 0)
    own = lane_group == (lax.shift_right_logical(mode, 6) & 1)
    chan = lax.broadcasted_iota(jnp.int32, (S5_GROUP, 1), 0)
    rep = jnp.where((lane & (S5_GROUP - 1)) == chan, 1.0, 0.0)
    b_r, b_i = _split_dot(br_ref[...], rep), _split_dot(bi_ref[...], rep)
    c_r, c_i = _split_dot(cr_ref[...], rep), _split_dot(ci_ref[...], rep)

    bbr, bbi = _cmul(f_re, f_im, b_r, b_i)

    p7r, p7i = powers(S5_STEP - 1 - step)
    w1r, w1i = _cmul(p7r, p7i, bbr, bbi)
    w1r_ref[...] = jnp.where(own, w1r, 0.0).astype(BF16)
    w1i_ref[...] = jnp.where(own, w1i, 0.0).astype(BF16)
    p1r, p1i = powers(step + 1)
    xr, xi = _cmul(p1r, p1i, c_r, c_i)
    w3xr_ref[...] = jnp.where(own, xr, 0.0).astype(BF16)
    w3xi_ref[...] = jnp.where(own, -xi, 0.0).astype(BF16)

    p0r, p0i = powers(step)
    car, cai = _cmul(p0r, p0i, c_r, c_i)
    car, cai = jnp.where(own, car, 0.0), jnp.where(own, cai, 0.0)
    lg, lig = lrg_ref[...], lig_ref[...]
    dtg = jnp.exp(ldtg_ref[...])
    mg = jnp.exp(lg * dtg)
    nrg, nig = mg * jnp.cos(lig * dtg) - 1.0, mg * jnp.sin(lig * dtg)
    deng = lg * lg + lig * lig
    fg_re = ((nrg * lg + nig * lig) / deng)[:, None, :]
    fg_im = ((nig * lg - nrg * lig) / deng)[:, None, :]
    bt_r, bt_i = _cmul(fg_re, fg_im, btr_ref[...], bti_ref[...])
    per_group = lambda a: a.reshape(S5_GROUPS, S5_STATE, MXU_TILE)
    taps = (_split_bmm(bt_r, per_group(car)) - _split_bmm(bt_i, per_group(cai))).reshape(S5_WIDTH, MXU_TILE)
    for i in range(S5_STEP):
        blk = _shift_lanes_256(taps, PAIR_CH * i)
        w3u_ref[:, PAIR_CH * i:PAIR_CH * (i + 1), :] = blk.reshape(S5_PAIRS, PAIR_CH, MXU_TILE).astype(BF16)

    pos = lax.broadcasted_iota(jnp.int32, (SUBLANES, 1), 0)
    for s, kstep in enumerate((1, 2, 4)):
        kr, ki = powers_rows(jnp.full((1, 1), float(S5_STEP * kstep), F32))
        tab_ref[2 * s] = jnp.where(pos >= kstep, kr, 0.0)
        tab_ref[2 * s + 1] = jnp.where(pos >= kstep, ki, 0.0)
    tab_ref[6], tab_ref[7] = powers_rows(((pos + 1) * S5_STEP).astype(F32))


def _s5_prep(lam_re, lam_im, log_dt, b_re, b_im, c_re, c_im):
    m = S5_MODES
    ldt = jnp.broadcast_to(log_dt.astype(F32)[:, None], (S5_GROUPS, S5_STATE))
    rowv = lambda a: a.astype(F32).reshape(1, m)
    bmat = lambda a: a.astype(F32).reshape(m, S5_GROUP)
    cmat = lambda a: a.astype(F32).transpose(0, 2, 1).reshape(m, S5_GROUP)
    tile = jax.ShapeDtypeStruct((m, MXU_TILE), BF16)
    out_shape = [tile] * 4 + [jax.ShapeDtypeStruct((S5_PAIRS, MXU_TILE, MXU_TILE), BF16),
                              jax.ShapeDtypeStruct((N_SCAN_TABLES, SUBLANES, m), F32)]
    w1r, w1i, w3xr, w3xi, w3u, tab = pl.pallas_call(
        _s5_prep_kernel,
        out_shape=out_shape,
        compiler_params=pltpu.CompilerParams(vmem_limit_bytes=VMEM_LIMIT),
        name="s5_prep",
    )(rowv(lam_re), rowv(lam_im), rowv(ldt), lam_re.astype(F32), lam_im.astype(F32), ldt,
      bmat(b_re), bmat(b_im), cmat(c_re), cmat(c_im),
      b_re.astype(F32).transpose(0, 2, 1), b_im.astype(F32).transpose(0, 2, 1))
    per_pair = lambda a: a.reshape(S5_PAIRS, PAIR_MODES, MXU_TILE)
    return per_pair(w1r), per_pair(w1i), per_pair(w3xr), per_pair(w3xi), w3u, tab


def _lane_window(lo):
    lane = lax.broadcasted_iota(jnp.int32, (1, LANES), 1)
    return (lane >= lo) & (lane < lo + PAIR_CH)


def _merge_windows(pieces):
    acc = pieces[0]
    for k in range(1, len(pieces)):
        acc = jnp.where(_lane_window(PAIR_CH * k), pieces[k], acc)
    return acc


def _rolled(cache, key, make, shift):
    if (key, shift) not in cache:
        x = make()
        cache[(key, shift)] = x if shift == 0 else pltpu.roll(x, shift, axis=1)
    return cache[(key, shift)]


def _s5_regroup_in(step_slab):
    per_tile = LANES // PAIR_CH
    cache, outs = {}, []
    for q in range(S5_PAIRS):
        halves = []
        for m in range(MXU_TILE // LANES):
            pieces = []
            for i in range(per_tile * m, per_tile * (m + 1)):
                c = q // per_tile
                pieces.append(_rolled(cache, (i, c), lambda i=i, c=c: step_slab(i, c),
                                      PAIR_CH * ((i - q) % per_tile)))
            halves.append(_merge_windows(pieces))
        outs.append(jnp.concatenate(halves, axis=1))
    return outs


def _s5_regroup_out(ys, j):
    per_tile = LANES // PAIR_CH
    m = j // per_tile
    slabs = []
    for c in range(S5_SLABS):
        pieces = [ys[q][:, LANES * m:LANES * (m + 1)] for q in range(per_tile * c, per_tile * (c + 1))]
        shifts = [PAIR_CH * ((q - j) % per_tile) for q in range(per_tile * c, per_tile * (c + 1))]
        slabs.append(_merge_windows([x if sh == 0 else pltpu.roll(x, sh, axis=1) for x, sh in zip(pieces, shifts)]))
    return slabs


def _mix_stages(q_ref, k_ref, la_ref, v_ref, r_ref, u5_ref, st0_ref, s50_ref, gh_ref,
                w1r_ref, w1i_ref, w3xr_ref, w3xi_ref, w3u_ref, d_ref, tab_ref,
                og_ref, g5_ref, sto_ref, s5o_ref,
                st_sc, car_sc, slab_sc, *, tb, streams, first):
    nm = S5_MODES
    nch = tb // CHUNK
    crow = lambda c: slice(c * CHUNK, (c + 1) * CHUNK)
    ti = lax.broadcasted_iota(jnp.int32, (CHUNK, CHUNK), 0)
    si = lax.broadcasted_iota(jnp.int32, (CHUNK, CHUNK), 1)
    tril = jnp.where(si <= ti, 1.0, 0.0).astype(BF16)
    lane = lax.broadcasted_iota(jnp.int32, (1, GLA_KEY), 1) % LANES
    head_mask = (lane < GLA_DK, lane >= GLA_DK)
    nsub = CHUNK // SUB
    npairs = GLA_HEADS // 2
    pt = lax.broadcasted_iota(jnp.int32, (2 * CHUNK, LANES), 0) & (CHUNK - 1)
    ps = lax.broadcasted_iota(jnp.int32, (2 * CHUNK, LANES), 1)
    causal_pair = ps <= pt
    sr = lax.broadcasted_iota(jnp.int32, (2 * GLA_DV, LANES), 0)
    sl = lax.broadcasted_iota(jnp.int32, (2 * GLA_DV, LANES), 1)
    own_head = (sr < GLA_DV) == (sl < GLA_DK)

    def row_block(x, i):
        parts = [jnp.zeros((SUB, x.shape[1]), x.dtype)] * nsub
        parts[i] = x[SUB * i:SUB * (i + 1)]
        return jnp.concatenate(parts, axis=0)

    bs = []
    for c in range(nch):
        la = la_ref[crow(c), :]
        hi = la.astype(BF16)
        lo = (la - hi.astype(F32)).astype(BF16)
        bs.append(_dot(tril, hi) + _dot(tril, lo))
    yield

    pms, qds, upds, els = [], [], [], []
    for c in range(nch):
        b = bs[c]
        refs = [jnp.zeros((1, GLA_KEY), F32)] + [b[SUB * i - 1:SUB * i, :] for i in range(1, nsub)]
        rb = jnp.concatenate([jnp.broadcast_to(x, (SUB, GLA_KEY)) for x in refs], axis=0)
        q, k = q_ref[crow(c), :], k_ref[crow(c), :]
        b_last = b[CHUNK - 1:CHUNK, :]
        qt = q * jnp.exp(b - rb)
        qt = [jnp.where(m, qt, 0.0).astype(BF16) for m in head_mask]
        qds.append((q * jnp.exp(b)).astype(BF16))
        kd = (k * jnp.exp(b_last - b)).astype(BF16)
        els.append(jnp.exp(b_last))
        kts = [(k * jnp.exp(jnp.minimum(x - b, EXP_CLAMP))).astype(BF16) for x in refs]
        pm_c, upd_c = [], []
        for pp in range(npairs):
            pc = slice(LANES * pp, LANES * (pp + 1))
            lhs = jnp.concatenate(
                [jnp.concatenate([row_block(qt[e][:, pc], i) for i in range(nsub)], axis=1) for e in (0, 1)], axis=0)
            rhs = jnp.concatenate([kts[i][:, pc] for i in range(nsub)], axis=1)
            rhs = jnp.concatenate([rhs, jnp.zeros_like(rhs)], axis=0)
            pm_c.append(jnp.where(causal_pair, _dot_nt(lhs, rhs), 0.0).astype(BF16))
            vt = v_ref[crow(c), 2 * GLA_DV * pp:2 * GLA_DV * (pp + 1)].astype(F32).T.astype(BF16)
            upd_c.append(_dot(vt, kd[:, pc]))
        pms.append(pm_c)
        upds.append(upd_c)
    r = r_ref[...]
    silu_r = r * _sigmoid(r)
    yield

    for pp in range(npairs):
        pc = slice(LANES * pp, LANES * (pp + 1))
        vc = slice(2 * GLA_DV * pp, 2 * GLA_DV * (pp + 1))
        if not streams:
            st = jnp.where(first, st0_ref[0, pp], st_sc[pp])
        for c in range(nch):
            if streams:
                st = st0_ref[0, c * npairs + pp]
            vp = v_ref[crow(c), vc]
            z = jnp.zeros((CHUNK, GLA_DV), BF16)
            zz = jnp.zeros((CHUNK, 2 * GLA_DV), BF16)
            v_diag = jnp.concatenate([jnp.concatenate([vp[:, :GLA_DV], z], axis=1), zz,
                                      jnp.concatenate([z, vp[:, GLA_DV:]], axis=1), zz], axis=0)
            pm = pms[c][pp]
            o = (_dot(jnp.concatenate([pm[:CHUNK], pm[CHUNK:]], axis=1), v_diag)
                 + _dot_nt(qds[c][:, pc], st.astype(BF16)))
            slab_sc[2 * pp, crow(c), :] = o[:, :GLA_DV]
            slab_sc[2 * pp + 1, crow(c), :] = o[:, GLA_DV:]
            st = jnp.where(own_head, st * els[c][:, pc] + upds[c][pp], 0.0)
            if streams:
                sto_ref[0, c * npairs + pp] = st
        if not streams:
            st_sc[pp] = st
    if not streams:
        sto_ref[0] = st_sc[...]
    tg = tb // S5_STEP
    ng = tg // SUBLANES
    step_slab = lambda i, c: u5_ref[c, pl.ds(i, tg, stride=S5_STEP), :]
    uq = [x.astype(BF16) for x in _s5_regroup_in(step_slab)]
    w = [_dot_nt(uq[q], jnp.concatenate([w1r_ref[q], w1i_ref[q]], axis=0)) for q in range(S5_PAIRS)]
    skip = [[d_ref[:, LANES * c:LANES * (c + 1)] * step_slab(j, c) for c in range(S5_SLABS)] for j in range(S5_STEP)]
    yield

    normed = []
    for h in range(GLA_HEADS):
        oh = slab_sc[h]
        normed.append(oh * lax.rsqrt(jnp.mean(oh * oh, axis=-1, keepdims=True) + EPS))
    og_ref[...] = (jnp.concatenate(normed, axis=1) * gh_ref[...] * silu_r).astype(BF16)
    yield

    xr = jnp.concatenate([x[:, :LANES] for x in w], axis=1).reshape(ng, SUBLANES, nm)
    xi = jnp.concatenate([x[:, LANES:] for x in w], axis=1).reshape(ng, SUBLANES, nm)
    for s, kstep in enumerate((1, 2, 4)):
        ar, ai = tab_ref[2 * s], tab_ref[2 * s + 1]
        sre = pltpu.roll(xr, kstep, axis=1)
        sim = pltpu.roll(xi, kstep, axis=1)
        xr, xi = xr + ar * sre - ai * sim, xi + ar * sim + ai * sre
    pr, pi = tab_ref[6], tab_ref[7]
    first_row = lax.broadcasted_iota(jnp.int32, (SUBLANES, nm), 0) == 0
    if not streams:
        cr = jnp.where(first, jnp.broadcast_to(s50_ref[0, :, :nm], (SUBLANES, nm)), car_sc[:, :nm])
        ci = jnp.where(first, jnp.broadcast_to(s50_ref[0, :, nm:], (SUBLANES, nm)), car_sc[:, nm:])
    prev_r, prev_i = [], []
    for g in range(ng):
        if streams:
            cr = jnp.broadcast_to(s50_ref[0, g:g + 1, :nm], (SUBLANES, nm))
            ci = jnp.broadcast_to(s50_ref[0, g:g + 1, nm:], (SUBLANES, nm))
        yr = xr[g] + pr * cr - pi * ci
        yi = xi[g] + pr * ci + pi * cr
        prev_r.append(jnp.where(first_row, cr, pltpu.roll(yr, 1, axis=0)))
        prev_i.append(jnp.where(first_row, ci, pltpu.roll(yi, 1, axis=0)))
        if streams:
            s5o_ref[0, g:g + 1, :nm] = yr[SUBLANES - 1:SUBLANES]
            s5o_ref[0, g:g + 1, nm:] = yi[SUBLANES - 1:SUBLANES]
        else:
            cr = jnp.broadcast_to(yr[SUBLANES - 1:SUBLANES], (SUBLANES, nm))
            ci = jnp.broadcast_to(yi[SUBLANES - 1:SUBLANES], (SUBLANES, nm))
    if not streams:
        car_sc[:, :nm] = cr
        car_sc[:, nm:] = ci
        s5o_ref[0] = car_sc[0:1, :]
    prev_r = jnp.concatenate(prev_r, axis=0).astype(BF16)
    prev_i = jnp.concatenate(prev_i, axis=0).astype(BF16)
    ys = []
    for q in range(S5_PAIRS):
        ms = slice(PAIR_MODES * q, PAIR_MODES * (q + 1))
        lhs = jnp.concatenate([prev_r[:, ms], prev_i[:, ms], uq[q]], axis=1)
        ys.append(_dot(lhs, jnp.concatenate([w3xr_ref[q], w3xi_ref[q], w3u_ref[q]], axis=0)))
    yield

    for j in range(S5_STEP):
        conv = _s5_regroup_out(ys, j)
        for c in range(S5_SLABS):
            y = conv[c] + skip[j][c]
            inner = math.sqrt(2.0 / math.pi) * (y + 0.044715 * (y * y * y))
            slab_sc[c, pl.ds(j, tg, stride=S5_STEP), :] = 0.5 * y * (1.0 + jnp.tanh(inner))
        if j % 3 == 2:
            yield
    g5_ref[...] = jnp.concatenate([slab_sc[c] for c in range(S5_SLABS)], axis=1).astype(BF16)


def _fm_kernel(x_ref, g1_ref, wg_ref, wu_ref, wd_ref, gmix_ref, win_ref, wga_ref, wgu_ref, bg_ref,
               st0_ref, s50_ref, gh_ref, w1r_ref, w1i_ref, w3xr_ref, w3xi_ref, w3u_ref, d_ref, tab_ref,
               h_ref, gmg_ref, gms_ref, og_ref, g5_ref, sto_ref, s5o_ref,
               q_sc, k_sc, la_sc, v_sc, r_sc, u5_sc, st_sc, car_sc, slab_sc, *, tile, sub, seq_tiles, streams):
    s = pl.program_id(0)

    @pl.when(s == 0)
    def _():
        for ref in (q_sc, k_sc, la_sc, v_sc, r_sc, u5_sc, st_sc, car_sc):
            ref[...] = jnp.zeros(ref.shape, ref.dtype)

    first = lax.rem(jnp.maximum(s - 1, 0), seq_tiles) == 0
    mix = _mix_stages(q_sc, k_sc, la_sc, v_sc, r_sc, u5_sc, st0_ref, s50_ref, gh_ref,
                      w1r_ref, w1i_ref, w3xr_ref, w3xi_ref, w3u_ref, d_ref, tab_ref,
                      og_ref, g5_ref, sto_ref, s5o_ref, st_sc, car_sc, slab_sc,
                      tb=tile, streams=streams, first=first)

    dests = dict(h=(h_ref, F32), gmg=(gmg_ref, BF16), gms=(gms_ref, BF16), q=(q_sc, F32), k=(k_sc, F32),
                 la=(la_sc, F32), v=(v_sc, BF16), r=(r_sc, F32))

    def sink(rows, f):
        for name, val in f.items():
            if name == 'u5':
                for c in range(S5_SLABS):
                    u5_sc[c, rows, :] = val[:, LANES * c:LANES * (c + 1)]
            else:
                ref, dt = dests[name]
                ref[rows, :] = val.astype(dt)

    front = _front_stages(x_ref, [slice(r0, r0 + sub) for r0 in range(0, tile, sub)], g1_ref, wg_ref, wu_ref, wd_ref,
                          gmix_ref, win_ref, wga_ref, wgu_ref, bg_ref, sink)
    for _ in front:
        next(mix, None)
    for _ in mix:
        pass


def _front_mix(x2d, st0, s50, p, tile, sub, seq_tiles, streams):
    n = x2d.shape[0]
    ntiles = n // tile
    cur = lambda s: jnp.minimum(s, ntiles - 1)
    prev = lambda s: jnp.maximum(s - 1, 0)
    row_cur = lambda w: pl.BlockSpec((tile, w), lambda s: (cur(s), 0))
    row_prev = lambda w: pl.BlockSpec((tile, w), lambda s: (prev(s), 0))
    st_spec = pl.BlockSpec((1,) + st0.shape[1:], lambda s: (prev(s) // seq_tiles, 0, 0, 0))
    s5_spec = pl.BlockSpec((1,) + s50.shape[1:], lambda s: (prev(s) // seq_tiles, 0, 0))
    pair_tile = _const_spec((S5_PAIRS, PAIR_MODES, MXU_TILE))
    in_specs = [row_cur(D_MODEL), _const_spec((1, D_MODEL)),
                _const_spec((D_MODEL, D_FF)), _const_spec((D_MODEL, D_FF)), _const_spec((D_FF, D_MODEL)),
                _const_spec((1, D_MODEL)), _const_spec((D_MODEL, W_MAIN_COLS)),
                _const_spec((D_MODEL, LANES)), _const_spec((LANES, GLA_KEY)), _const_spec((1, GLA_KEY)),
                st_spec, s5_spec, _const_spec((1, GLA_VAL)),
                pair_tile, pair_tile, pair_tile, pair_tile, _const_spec((S5_PAIRS, MXU_TILE, MXU_TILE)),
                _const_spec((1, S5_WIDTH)), _const_spec((N_SCAN_TABLES, SUBLANES, S5_MODES))]
    sds = lambda w, dt: jax.ShapeDtypeStruct((n, w), dt)
    out_shape = [sds(D_MODEL, F32), sds(D_MODEL, BF16), sds(D_MODEL, BF16), sds(GLA_VAL, BF16), sds(S5_WIDTH, BF16),
                 jax.ShapeDtypeStruct(st0.shape, F32), jax.ShapeDtypeStruct(s50.shape, F32)]
    out_specs = [row_cur(D_MODEL), row_cur(D_MODEL), row_cur(D_MODEL), row_prev(GLA_VAL), row_prev(S5_WIDTH),
                 st_spec, s5_spec]
    scratch = [pltpu.VMEM((tile, GLA_KEY), F32), pltpu.VMEM((tile, GLA_KEY), F32), pltpu.VMEM((tile, GLA_KEY), F32),
               pltpu.VMEM((tile, GLA_VAL), BF16), pltpu.VMEM((tile, GLA_VAL), F32),
               pltpu.VMEM((S5_SLABS, tile, LANES), F32),
               pltpu.VMEM((GLA_HEADS // 2, 2 * GLA_DV, LANES), F32), pltpu.VMEM((SUBLANES, 2 * S5_MODES), F32),
               pltpu.VMEM((S5_SLABS, tile, LANES), F32)]
    return pl.pallas_call(
        functools.partial(_fm_kernel, tile=tile, sub=sub, seq_tiles=seq_tiles, streams=streams),
        out_shape=out_shape,
        grid=(ntiles + 1,),
        in_specs=in_specs,
        out_specs=out_specs,
        scratch_shapes=scratch,
        compiler_params=pltpu.CompilerParams(dimension_semantics=("arbitrary",), vmem_limit_bytes=VMEM_LIMIT),
        name="front_mix",
    )(x2d, p['g_ffn1'], p['wg1'], p['wu1'], p['wd1'], p['g_mix'], p['w_main'], p['w_ga'], p['w_gu'], p['b_gate'],
      st0, s50, p['g_gla_head'], *p['s5_tiles'], p['s5_d'], p['tab'])


def _back_kernel(h_ref, og_ref, g5_ref, gmg_ref, gms_ref, wgo_ref, wa_ref, wb_ref, wso_ref, wo_ref,
                 g2_ref, wg_ref, wu_ref, wd_ref, gf_ref, y_ref, *, sub):
    blocks = [slice(r0, r0 + sub) for r0 in range(0, h_ref.shape[0], sub)]
    y_gla = [_dot(og_ref[rows, :], wgo_ref[...]) for rows in blocks]
    ab = [(_dot(g5_ref[rows, :], wa_ref[...]), _dot(g5_ref[rows, :], wb_ref[...])) for rows in blocks]
    y_s5 = [_dot((a * _sigmoid(b)).astype(BF16), wso_ref[...]) for a, b in ab]
    hs = []
    for rows, yg, ys in zip(blocks, y_gla, y_s5):
        m = _sigmoid(gmg_ref[rows, :].astype(F32)) * yg + _sigmoid(gms_ref[rows, :].astype(F32)) * ys
        hs.append(h_ref[rows, :] + _dot(m.astype(BF16), wo_ref[...]))
    accs = [None] * len(blocks)
    for _ in _swiglu_steps([_rms(h, g2_ref[...]).astype(BF16) for h in hs], wg_ref, wu_ref, wd_ref, accs):
        pass
    for rows, h, acc in zip(blocks, hs, accs):
        y_ref[rows, :] = _rms(h + 0.5 * acc, gf_ref[...])


def _back(h, og, g5, gmg, gms, p, tm, sub):
    n = h.shape[0]
    row = lambda w: pl.BlockSpec((tm, w), lambda i: (i, 0))
    in_specs = [row(D_MODEL), row(GLA_VAL), row(S5_WIDTH), row(D_MODEL), row(D_MODEL),
                _const_spec((GLA_VAL, D_MODEL)), _const_spec((S5_WIDTH, S5_WIDTH)), _const_spec((S5_WIDTH, S5_WIDTH)),
                _const_spec((S5_WIDTH, D_MODEL)), _const_spec((D_MODEL, D_MODEL)), _const_spec((1, D_MODEL)),
                _const_spec((D_MODEL, D_FF)), _const_spec((D_MODEL, D_FF)), _const_spec((D_FF, D_MODEL)),
                _const_spec((1, D_MODEL))]
    return pl.pallas_call(
        functools.partial(_back_kernel, sub=sub),
        out_shape=jax.ShapeDtypeStruct((n, D_MODEL), F32),
        grid=(n // tm,),
        in_specs=in_specs,
        out_specs=row(D_MODEL),
        compiler_params=pltpu.CompilerParams(dimension_semantics=("parallel",), vmem_limit_bytes=VMEM_LIMIT),
        name="back",
    )(h, og, g5, gmg, gms, p['w_gla_out'], p['w_glu_a'], p['w_glu_b'], p['w_s5_out'], p['w_out'],
      p['g_ffn2'], p['wg2'], p['wu2'], p['wd2'], p['g_final'])


def _gla_state_in(s, nb):
    st = jnp.swapaxes(s.astype(F32), 2, 3)
    z = jnp.zeros_like(st)
    sel = (jnp.arange(GLA_HEADS) % 2 == 0)[None, :, None, None]
    st = jnp.where(sel, jnp.concatenate([st, z], axis=-1), jnp.concatenate([z, st], axis=-1))
    return st.reshape(nb, -1, 2 * GLA_DV, LANES)


def _gla_state_out(st, bsz):
    st = st.reshape(bsz, GLA_HEADS, GLA_DV, LANES)
    sel = (jnp.arange(GLA_HEADS) % 2 == 0)[None, :, None, None]
    return jnp.swapaxes(jnp.where(sel, st[..., :GLA_DK], st[..., GLA_DK:]), 2, 3)


def _s5_state_in(x0, nb):
    x = x0.astype(F32).reshape(x0.shape[0], S5_MODES, 2)
    return jnp.concatenate([x[..., 0], x[..., 1]], axis=-1).reshape(nb, -1, 2 * S5_MODES)


def _s5_state_out(x, bsz):
    x = x.reshape(bsz, 2, S5_MODES)
    return jnp.stack([x[:, 0], x[:, 1]], axis=-1).reshape(bsz, S5_GROUPS, S5_STATE, 2)


def _layer(x, s_gla, s_s5, p, tile, sub, streams):
    bsz, t, _ = x.shape
    n = bsz * t
    seq_tiles = 1 if streams else t // tile
    nblk = n // (tile * seq_tiles)
    h, gmg, gms, og, g5, st, s5 = _front_mix(x.reshape(n, D_MODEL), _gla_state_in(s_gla, nblk),
                                             _s5_state_in(s_s5, nblk), p, tile, sub, seq_tiles, streams)
    y = _back(h, og, g5, gmg, gms, p, BACK_TILE, BACK_TILE // 2)
    return y.reshape(bsz, t, D_MODEL), _gla_state_out(st, bsz), _s5_state_out(s5, bsz)


def kernel(x_prompt, x_sample, state_gla, state_s5, g_ffn1, w_ffn1_gate, w_ffn1_up, w_ffn1_down, g_mix, w_in, w_gate_up, b_gate, g_gla_head, w_gla_out, s5_lam_re, s5_lam_im, s5_log_dt, s5_b_re, s5_b_im, s5_c_re, s5_c_im, s5_d, w_glu_a, w_glu_b, w_s5_out, w_out, g_ffn2, w_ffn2_gate, w_ffn2_up, w_ffn2_down, g_final):
    vec = lambda a: a.astype(F32).reshape(1, -1)
    sizes = (GLA_KEY, GLA_KEY, GLA_VAL, GLA_VAL, GLA_GATE_RANK, S5_WIDTH, D_MODEL, D_MODEL)
    offs = [sum(sizes[:i]) for i in range(len(sizes) + 1)]
    wga = w_in[:, offs[4]:offs[5]]
    w_main = jnp.concatenate([w_in[:, :offs[4]], w_in[:, offs[5]:]], axis=1)
    *s5_tiles, tab = _s5_prep(s5_lam_re, s5_lam_im, s5_log_dt, s5_b_re, s5_b_im, s5_c_re, s5_c_im)
    p = {
        'g_ffn1': vec(g_ffn1), 'wg1': w_ffn1_gate.astype(BF16), 'wu1': w_ffn1_up.astype(BF16),
        'wd1': w_ffn1_down.astype(BF16), 'g_mix': vec(g_mix),
        'w_main': w_main.astype(BF16),
        'w_ga': jnp.pad(wga, ((0, 0), (0, LANES - GLA_GATE_RANK))).astype(BF16),
        'w_gu': jnp.pad(w_gate_up, ((0, LANES - GLA_GATE_RANK), (0, 0))).astype(BF16),
        'b_gate': vec(b_gate), 'g_gla_head': vec(g_gla_head),
        's5_tiles': s5_tiles, 'tab': tab, 's5_d': vec(s5_d),
        'w_gla_out': w_gla_out.astype(BF16), 'w_glu_a': w_glu_a.astype(BF16), 'w_glu_b': w_glu_b.astype(BF16),
        'w_s5_out': w_s5_out.astype(BF16), 'w_out': w_out.astype(BF16),
        'g_ffn2': vec(g_ffn2), 'wg2': w_ffn2_gate.astype(BF16), 'wu2': w_ffn2_up.astype(BF16),
        'wd2': w_ffn2_down.astype(BF16), 'g_final': vec(g_final),
    }
    bp = x_prompt.shape[0]
    zero_gla = jnp.zeros((bp, GLA_HEADS, GLA_DK, GLA_DV), F32)
    zero_s5 = jnp.zeros((bp, S5_GROUPS, S5_STATE, 2), F32)
    y_p, gla_p, s5_p = _layer(x_prompt, zero_gla, zero_s5, p, tile=PROMPT_TILE, sub=PROMPT_TILE // 2, streams=False)
    y_s, gla_s, s5_s = _layer(x_sample, state_gla, state_s5, p, tile=STREAM_TILE, sub=STREAM_TILE // 2, streams=True)
    return (y_p, y_s, gla_p, s5_p, gla_s, s5_s)
```
